```python
import jax, jax.numpy as jnp
from jax import lax
import numpy as np

D_MODEL = 1024
BATCH = 8
SEQ = 2048
DEPTH = 1
DEC_BATCH = 128
DEC_SEQ = 4
PAST_LEN = 16384
PAGE_SIZE = 128

N_META = 16
C_CONV = D_MODEL
K_CONV = 31
N_HEADS = 8
D_HEAD = 128
D_QK = N_HEADS * D_HEAD
D_QKV = 3 * D_QK
K_SHORT = 4
CHUNK = 64
D_FF = 4 * D_MODEL
EPS = 1e-6
D_IN = 2 * C_CONV + D_QKV + D_QK + 2 * N_HEADS + 2 * D_MODEL

kernel_name = "conformer_gdn_parallel_hybrid_step"


def rms_norm(x, g):
    xf = x.astype(jnp.float32)
    y = xf * lax.rsqrt(jnp.mean(xf * xf, axis=-1, keepdims=True) + EPS)
    return (y * g.astype(jnp.float32)).astype(x.dtype)


def layer_norm(x, g, b):
    xf = x.astype(jnp.float32)
    mu = jnp.mean(xf, axis=-1, keepdims=True)
    var = jnp.mean(jnp.square(xf - mu), axis=-1, keepdims=True)
    y = (xf - mu) * lax.rsqrt(var + EPS)
    return (y * g.astype(jnp.float32) + b.astype(jnp.float32)).astype(x.dtype)


def l2_normalize(x):
    return x * lax.rsqrt(jnp.sum(x * x, axis=-1, keepdims=True) + 1e-6)


def causal_depthwise_conv(x, buf, w):
    ext = jnp.concatenate([buf.astype(x.dtype), x], axis=1)
    y = lax.conv_general_dilated(ext, w[:, None, :].astype(x.dtype), window_strides=(1,),
                                 padding='VALID', dimension_numbers=('NWC', 'WIO', 'NWC'),
                                 feature_group_count=x.shape[-1])
    return y, ext[:, -(w.shape[0] - 1):]


def gdn_intra(q, k, v, beta, g):
    L = q.shape[-2]
    G = jnp.cumsum(g, axis=-1)
    pos = jnp.arange(L)
    causal = pos[:, None] >= pos[None, :]
    decay = jnp.exp(jnp.where(causal, G[..., :, None] - G[..., None, :], -jnp.inf))
    M = beta[..., :, None] * jnp.einsum('...id,...jd->...ij', k, k) * decay
    A = jnp.tril(M, -1) + jnp.eye(L, dtype=M.dtype)
    rhs = jnp.concatenate([v * beta[..., None], k * (beta * jnp.exp(G))[..., None]], axis=-1)
    sol = lax.linalg.triangular_solve(A, rhs, left_side=True, lower=True, unit_diagonal=True)
    value, kcum = sol[..., :v.shape[-1]], sol[..., v.shape[-1]:]
    attn = jnp.einsum('...id,...jd->...ij', q, k) * decay
    q_dec = q * jnp.exp(G)[..., None]
    g_last = G[..., -1]
    k_dec = k * jnp.exp(g_last[..., None] - G)[..., None]
    return value, kcum, attn, q_dec, k_dec, jnp.exp(g_last)


def gdn_inter(S, value, kcum, attn, q_dec, k_dec, last):
    u = value - jnp.einsum('...ld,...de->...le', kcum, S)
    o = jnp.einsum('...ld,...de->...le', q_dec, S) + jnp.einsum('...ij,...je->...ie', attn, u)
    S = S * last[..., None, None] + jnp.einsum('...ld,...le->...de', k_dec, u)
    return S, o


def gated_delta(q, k, v, beta, g, S0, n_lead, chunk):
    B, H, T, _ = q.shape
    S = S0
    outs = []
    if n_lead > 0:
        S, o_lead = gdn_inter(S, *gdn_intra(q[:, :, :n_lead], k[:, :, :n_lead], v[:, :, :n_lead],
                                             beta[:, :, :n_lead], g[:, :, :n_lead]))
        outs.append(o_lead)
    nc = (T - n_lead) // chunk

    def blocks(a):
        a = a[:, :, n_lead:]
        a = a.reshape((B, H, nc, chunk) + a.shape[3:])
        return jnp.moveaxis(a, 2, 0)

    intra = gdn_intra(blocks(q), blocks(k), blocks(v), blocks(beta), blocks(g))
    S, o = lax.scan(lambda s, xs: gdn_inter(s, *xs), S, intra)
    o = jnp.moveaxis(o, 0, 2).reshape(B, H, nc * chunk, v.shape[-1])
    outs.append(o)
    return jnp.concatenate(outs, axis=2), S


def layer(x, conv_buf, qkv_buf, S0, n_lead, chunk, g_mix, w_in, w_dw, b_dw, ln_g, ln_b, w_cout,
          w_short, a_log, dt_bias, g_head, w_dout, w_o, g_mlp, w_up, w_down):
    Bx, T, _ = x.shape
    f32 = jnp.float32
    h = rms_norm(x, g_mix)
    proj = h @ w_in
    cuts = np.cumsum([C_CONV, C_CONV, D_QKV, D_QK, N_HEADS, N_HEADS, D_MODEL]).tolist()
    glu_a, glu_b, qkv, z, b_raw, a_raw, gate_c, gate_d = jnp.split(proj, cuts, axis=-1)

    glu = glu_a * jax.nn.sigmoid(glu_b)
    c, new_conv = causal_depthwise_conv(glu, conv_buf, w_dw)
    c = jax.nn.silu(layer_norm(c + b_dw, ln_g, ln_b))
    conv_out = c @ w_cout

    qkv_c, new_qkv = causal_depthwise_conv(qkv, qkv_buf, w_short)
    qkv_c = jax.nn.silu(qkv_c).astype(f32)
    q, k, v = jnp.split(qkv_c, 3, axis=-1)
    heads = lambda a: a.reshape(Bx, T, N_HEADS, D_HEAD).transpose(0, 2, 1, 3)
    q = l2_normalize(heads(q)) * (D_HEAD ** -0.5)
    k = l2_normalize(heads(k))
    v = heads(v)
    beta = jax.nn.sigmoid(b_raw.astype(f32)).transpose(0, 2, 1)
    g = (-jnp.exp(a_log.astype(f32)) * jax.nn.softplus(a_raw.astype(f32) + dt_bias.astype(f32))).transpose(0, 2, 1)
    o, S_new = gated_delta(q, k, v, beta, g, S0.astype(f32), n_lead, chunk)
    o = o.transpose(0, 2, 1, 3)
    o = rms_norm(o, g_head) * jax.nn.silu(z.astype(f32).reshape(Bx, T, N_HEADS, D_HEAD))
    dn_out = o.reshape(Bx, T, D_QK).astype(x.dtype) @ w_dout

    mix = jax.nn.sigmoid(gate_c) * conv_out + jax.nn.sigmoid(gate_d) * dn_out
    x = x + mix @ w_o
    hm = rms_norm(x, g_mlp) @ w_up
    x = x + jnp.square(jax.nn.relu(hm)) @ w_down
    return x, new_conv, new_qkv, S_new.astype(S0.dtype)


def setup_inputs(seed: int = 0) -> dict:
    key = jax.random.key(seed)
    ks = jax.random.split(key, 24)
    f32 = jnp.float32
    nrm = lambda k, shape, s: jax.random.normal(k, shape, f32) * s
    return {
        "x_prompt": nrm(ks[0], (BATCH, SEQ, D_MODEL), 1.0),
        "x_sample": nrm(ks[1], (DEC_BATCH, DEC_SEQ, D_MODEL), 1.0),
        "state_conv": nrm(ks[2], (DEPTH, DEC_BATCH, K_CONV - 1, C_CONV), 0.5),
        "state_qkv_conv": nrm(ks[3], (DEPTH, DEC_BATCH, K_SHORT - 1, D_QKV), 1.0),
        "state_delta": nrm(ks[4], (DEPTH, DEC_BATCH, N_HEADS, D_HEAD, D_HEAD), 0.5),
        "meta_tokens": nrm(ks[5], (N_META, D_MODEL), 1.0),
        "g_mix": 1.0 + nrm(ks[6], (DEPTH, D_MODEL), 0.01),
        "w_in": nrm(ks[7], (DEPTH, D_MODEL, D_IN), D_MODEL ** -0.5),
        "w_dw": nrm(ks[8], (DEPTH, K_CONV, C_CONV), K_CONV ** -0.5),
        "b_dw": nrm(ks[9], (DEPTH, C_CONV), 0.01),
        "ln_g": 1.0 + nrm(ks[10], (DEPTH, C_CONV), 0.01),
        "ln_b": nrm(ks[11], (DEPTH, C_CONV), 0.01),
        "w_cout": nrm(ks[12], (DEPTH, C_CONV, D_MODEL), C_CONV ** -0.5),
        "w_short": nrm(ks[13], (DEPTH, K_SHORT, D_QKV), K_SHORT ** -0.5),
        "a_log": jnp.log(jax.random.uniform(ks[14], (DEPTH, N_HEADS), f32, 1.0, 16.0)),
        "dt_bias": nrm(ks[15], (DEPTH, N_HEADS), 0.5),
        "g_head": 1.0 + nrm(ks[16], (DEPTH, D_HEAD), 0.01),
        "w_dout": nrm(ks[17], (DEPTH, D_QK, D_MODEL), D_QK ** -0.5),
        "w_o": nrm(ks[18], (DEPTH, D_MODEL, D_MODEL), D_MODEL ** -0.5),
        "g_mlp": 1.0 + nrm(ks[19], (DEPTH, D_MODEL), 0.01),
        "w_up": nrm(ks[20], (DEPTH, D_MODEL, D_FF), D_MODEL ** -0.5),
        "w_down": nrm(ks[21], (DEPTH, D_FF, D_MODEL), D_FF ** -0.5),
        "g_final": 1.0 + nrm(ks[22], (D_MODEL,), 0.01),
    }


def reference(x_prompt, x_sample, state_conv, state_qkv_conv, state_delta, meta_tokens, g_mix, w_in,
              w_dw, b_dw, ln_g, ln_b, w_cout, w_short, a_log, dt_bias, g_head, w_dout, w_o, g_mlp,
              w_up, w_down, g_final):
    Bp = x_prompt.shape[0]
    meta = jnp.broadcast_to(meta_tokens.astype(x_prompt.dtype)[None], (Bp, N_META, D_MODEL))
    xp = jnp.concatenate([meta, x_prompt], axis=1)
    xs = x_sample
    pc_l, pq_l, pS_l, sc_l, sq_l, sS_l = [], [], [], [], [], []
    for l in range(DEPTH):
        params = (g_mix[l], w_in[l], w_dw[l], b_dw[l], ln_g[l], ln_b[l], w_cout[l], w_short[l],
                  a_log[l], dt_bias[l], g_head[l], w_dout[l], w_o[l], g_mlp[l], w_up[l], w_down[l])
        zc = jnp.zeros((Bp, K_CONV - 1, C_CONV), xp.dtype)
        zq = jnp.zeros((Bp, K_SHORT - 1, D_QKV), xp.dtype)
        zS = jnp.zeros((Bp, N_HEADS, D_HEAD, D_HEAD), state_delta.dtype)
        xp, pc, pq, pS = layer(xp, zc, zq, zS, N_META, CHUNK, *params)
        xs, sc, sq, sS = layer(xs, state_conv[l], state_qkv_conv[l], state_delta[l], 0, xs.shape[1], *params)
        pc_l.append(pc); pq_l.append(pq); pS_l.append(pS)
        sc_l.append(sc); sq_l.append(sq); sS_l.append(sS)
    y_prompt = rms_norm(xp[:, N_META:], g_final)
    y_sample = rms_norm(xs, g_final)
    prompt_conv = jnp.stack(pc_l)
    prompt_qkv_conv = jnp.stack(pq_l)
    prompt_delta = jnp.stack(pS_l)
    sample_conv = jnp.stack(sc_l)
    sample_qkv_conv = jnp.stack(sq_l)
    sample_delta = jnp.stack(sS_l)
    return (y_prompt, y_sample, prompt_conv, prompt_qkv_conv, prompt_delta, sample_conv, sample_qkv_conv, sample_delta)
```

```python
import functools

import jax
import jax.numpy as jnp
from jax import lax
from jax.experimental import pallas as pl
from jax.experimental.pallas import tpu as pltpu

F32 = jnp.float32
BF16 = jnp.bfloat16

D_MODEL = 1024
N_HEADS = 8
D_HEAD = 128
D_QKV = 3 * D_MODEL
D_FF = 4 * D_MODEL
K_CONV = 31
K_SHORT = 4
N_META = 16
EPS = 1e-6

CONV_HALO = 32
SHORT_HALO = 8
LANES = 128
COL_BLOCK = 512

OFF_GLU_A, OFF_GLU_B, OFF_QKV, OFF_Z, OFF_GC, OFF_GD = 0, 1024, 2048, 5120, 6144, 7168

VMEM_LIMIT = 56 * 1024 * 1024


def _dot(a, b):
    return jnp.dot(a, b, preferred_element_type=F32)


def _dot_nt(a, b):
    return lax.dot_general(a, b, (((1,), (1,)), ((), ())), preferred_element_type=F32)


def _dot_tn(a, b):
    return lax.dot_general(a, b, (((0,), (0,)), ((), ())), preferred_element_type=F32)


def _sigmoid(x):
    return 1.0 / (1.0 + jnp.exp(-x))


def _silu(x):
    return x * _sigmoid(x)


def _softplus(x):
    return jnp.maximum(x, 0.0) + jnp.log(1.0 + jnp.exp(-jnp.abs(x)))


def _rms(x, g):
    return x * lax.rsqrt(jnp.mean(x * x, axis=-1, keepdims=True) + EPS) * g


def _layer_norm_silu(c, g, b):
    mu = jnp.mean(c, axis=-1, keepdims=True)
    d = c - mu
    var = jnp.mean(d * d, axis=-1, keepdims=True)
    return _silu(d * lax.rsqrt(var + EPS) * g + b)


def _split3(x):
    x1 = x.astype(BF16)
    r1 = x - x1.astype(F32)
    x2 = r1.astype(BF16)
    x3 = (r1 - x2.astype(F32)).astype(BF16)
    return x1, x2, x3


def _resident(shape):
    nd = len(shape)
    return pl.BlockSpec(shape, lambda *_: (0,) * nd, pipeline_mode=pl.Buffered(1))


def _beta_g_rows(raw, alog_row, dt_row):
    lane = lax.broadcasted_iota(jnp.int32, raw.shape, 1)
    g = -jnp.exp(alog_row) * _softplus(raw + dt_row)
    return jnp.where(lane < N_HEADS, _sigmoid(raw), g)


def _qkv_post(acc, cb):
    y = _silu(acc)
    if cb < 2 * N_HEADS:
        y = y * lax.rsqrt(jnp.sum(y * y, axis=-1, keepdims=True) + 1e-6)
        if cb < N_HEADS:
            y = y * (D_HEAD ** -0.5)
    return y


def _mixer_in_seq_kernel(x_ref, cbuf_ref, qbuf_ref, gmix_ref, w_ref, wba_ref, wbat_ref, wdw_ref,
                         bdw_ref, lng_ref, lnb_ref, wcout_ref, wsh_ref, alr_ref, dtr_ref, alc_ref,
                         dtc_ref, a_ref, qkv_ref, gz_ref, sgd_ref, bg_ref, bgt_ref, ncv_ref, nqk_ref,
                         ext_ref, qext_ref, c_ref, *, tm, row_block):
    t = pl.program_id(1)
    nt = pl.num_programs(1)

    @pl.when(t == 0)
    def _():
        ext_ref[0:CONV_HALO, :] = cbuf_ref[0]
        qext_ref[0:SHORT_HALO, :] = qbuf_ref[0]

    if tm >= CONV_HALO:
        @pl.when(t > 0)
        def _():
            ext_ref[0:CONV_HALO, :] = ext_ref[tm:tm + CONV_HALO, :]
            qext_ref[0:SHORT_HALO, :] = qext_ref[tm:tm + SHORT_HALO, :]

    x = x_ref[0]
    h = _rms(x, gmix_ref[...]).astype(BF16)

    for j in range(0, D_MODEL, COL_BLOCK):
        a = _dot(h, w_ref[:, OFF_GLU_A + j:OFF_GLU_A + j + COL_BLOCK])
        b = _dot(h, w_ref[:, OFF_GLU_B + j:OFF_GLU_B + j + COL_BLOCK])
        ext_ref[CONV_HALO:CONV_HALO + tm, j:j + COL_BLOCK] = a * _sigmoid(b)

    first = CONV_HALO - (K_CONV - 1)
    for r in range(0, tm, row_block):
        acc = jnp.zeros((row_block, D_MODEL), F32) + bdw_ref[...]
        for i in range(K_CONV):
            acc = acc + wdw_ref[i:i + 1, :] * ext_ref[r + first + i:r + first + i + row_block, :]
        c_ref[r:r + row_block, :] = _layer_norm_silu(acc, lng_ref[...], lnb_ref[...]).astype(BF16)

    conv_out = _dot(c_ref[...], wcout_ref[...])
    gate_c = _dot(h, w_ref[:, OFF_GC:OFF_GC + D_MODEL])
    a_ref[0] = (_sigmoid(gate_c) * conv_out).astype(BF16)

    for j in range(0, D_QKV, COL_BLOCK):
        qext_ref[SHORT_HALO:SHORT_HALO + tm, j:j + COL_BLOCK] = _dot(
            h, w_ref[:, OFF_QKV + j:OFF_QKV + j + COL_BLOCK])
    sfirst = SHORT_HALO - (K_SHORT - 1)
    for cb in range(D_QKV // LANES):
        sl = slice(cb * LANES, (cb + 1) * LANES)
        acc = wsh_ref[0:1, sl] * qext_ref[sfirst:sfirst + tm, sl]
        for i in range(1, K_SHORT):
            acc = acc + wsh_ref[i:i + 1, sl] * qext_ref[sfirst + i:sfirst + i + tm, sl]
        qkv_ref[0, :, sl] = _qkv_post(acc, cb).astype(BF16)

    gz_ref[0] = _silu(_dot(h, w_ref[:, OFF_Z:OFF_Z + D_MODEL])).astype(BF16)
    sgd_ref[0] = _sigmoid(_dot(h, w_ref[:, OFF_GD:OFF_GD + D_MODEL])).astype(BF16)

    bg_ref[0] = _beta_g_rows(_dot(h, wba_ref[...]), alr_ref[...], dtr_ref[...])
    rawt = _dot_nt(wbat_ref[...], h)
    srow = lax.broadcasted_iota(jnp.int32, rawt.shape, 0)
    gt = -jnp.exp(alc_ref[...]) * _softplus(rawt + dtc_ref[...])
    bgt_ref[0] = jnp.where(srow < N_HEADS, _sigmoid(rawt), gt)

    @pl.when(t == nt - 1)
    def _():
        ncv_ref[0] = ext_ref[tm:tm + CONV_HALO, :]
        nqk_ref[0] = qext_ref[tm:tm + SHORT_HALO, :]


def _mixer_in_seq(x, cbuf, qbuf, p, *, tm, row_block):
    B, T, _ = x.shape
    nt = T // tm
    bs = cbuf.shape[0]
    state_idx = (lambda b, t: (b, 0, 0)) if bs == B else (lambda b, t: (0, 0, 0))
    tile = lambda n: pl.BlockSpec((1, tm, n), lambda b, t: (b, t, 0))
    in_specs = [
        tile(D_MODEL),
        pl.BlockSpec((1, CONV_HALO, D_MODEL), state_idx),
        pl.BlockSpec((1, SHORT_HALO, D_QKV), state_idx),
        _resident((1, D_MODEL)),
        _resident(p["wmain"].shape),
        _resident(p["wba"].shape),
        _resident(p["wbat"].shape),
        _resident(p["wdw"].shape),
        _resident((1, D_MODEL)),
        _resident((1, D_MODEL)),
        _resident((1, D_MODEL)),
        _resident(p["wcout"].shape),
        _resident(p["wsh"].shape),
        _resident((1, LANES)),
        _resident((1, LANES)),
        _resident((2 * N_HEADS, 1)),
        _resident((2 * N_HEADS, 1)),
    ]
    out_shape = [
        jax.ShapeDtypeStruct((B, T, D_MODEL), BF16),
        jax.ShapeDtypeStruct((B, T, D_QKV), BF16),
        jax.ShapeDtypeStruct((B, T, D_MODEL), BF16),
        jax.ShapeDtypeStruct((B, T, D_MODEL), BF16),
        jax.ShapeDtypeStruct((B, T, LANES), F32),
        jax.ShapeDtypeStruct((B, 2 * N_HEADS, T), F32),
        jax.ShapeDtypeStruct((B, CONV_HALO, D_MODEL), F32),
        jax.ShapeDtypeStruct((B, SHORT_HALO, D_QKV), F32),
    ]
    out_specs = [
        tile(D_MODEL), tile(D_QKV), tile(D_MODEL), tile(D_MODEL), tile(LANES),
        pl.BlockSpec((1, 2 * N_HEADS, tm), lambda b, t: (b, 0, t)),
        pl.BlockSpec((1, CONV_HALO, D_MODEL), lambda b, t: (b, 0, 0)),
        pl.BlockSpec((1, SHORT_HALO, D_QKV), lambda b, t: (b, 0, 0)),
    ]
    return pl.pallas_call(
        functools.partial(_mixer_in_seq_kernel, tm=tm, row_block=row_block),
        grid=(B, nt),
        in_specs=in_specs,
        out_specs=out_specs,
        out_shape=out_shape,
        scratch_shapes=[
            pltpu.VMEM((tm + CONV_HALO, D_MODEL), F32),
            pltpu.VMEM((tm + SHORT_HALO, D_QKV), F32),
            pltpu.VMEM((tm, D_MODEL), BF16),
        ],
        compiler_params=pltpu.CompilerParams(
            dimension_semantics=("arbitrary", "arbitrary"), vmem_limit_bytes=VMEM_LIMIT),
        name="mixer_in_seq",
    )(x, cbuf, qbuf, p["gmix"], p["wmain"], p["wba"], p["wbat"], p["wdw"], p["bdw"], p["lng"],
      p["lnb"], p["wcout"], p["wsh"], p["alog_row"], p["dt_row"], p["alog_col"], p["dt_col"])


def _mixer_in_dec_kernel(x_ref, cbuf_ref, qbuf_ref, gmix_ref, w_ref, wba_ref, wdw_ref, bdw_ref,
                         lng_ref, lnb_ref, wcout_ref, wsh_ref, alr_ref, dtr_ref,
                         a_ref, qkv_ref, gz_ref, sgd_ref, bg_ref, glu_ref, raw_ref, c_ref, *, ts, nb):
    n = ts * nb
    x = x_ref[...].reshape(n, D_MODEL)
    h = _rms(x, gmix_ref[...]).astype(BF16)

    for j in range(0, D_MODEL, COL_BLOCK):
        a = _dot(h, w_ref[:, OFF_GLU_A + j:OFF_GLU_A + j + COL_BLOCK])
        b = _dot(h, w_ref[:, OFF_GLU_B + j:OFF_GLU_B + j + COL_BLOCK])
        glu_ref[:, :, j:j + COL_BLOCK] = (a * _sigmoid(b)).reshape(ts, nb, COL_BLOCK)

    nbuf = K_CONV - 1
    for t in range(ts):
        acc = jnp.zeros((nb, D_MODEL), F32) + bdw_ref[...]
        for i in range(K_CONV):
            j = t + i
            row = cbuf_ref[j] if j < nbuf else glu_ref[j - nbuf]
            acc = acc + wdw_ref[i:i + 1, :] * row
        c_ref[t * nb:(t + 1) * nb, :] = _layer_norm_silu(acc, lng_ref[...], lnb_ref[...]).astype(BF16)

    conv_out = _dot(c_ref[...], wcout_ref[...])
    gate_c = _dot(h, w_ref[:, OFF_GC:OFF_GC + D_MODEL])
    a_ref[...] = (_sigmoid(gate_c) * conv_out).astype(BF16).reshape(ts, nb, D_MODEL)

    for j in range(0, D_QKV, COL_BLOCK):
        raw_ref[:, :, j:j + COL_BLOCK] = _dot(
            h, w_ref[:, OFF_QKV + j:OFF_QKV + j + COL_BLOCK]).reshape(ts, nb, COL_BLOCK)
    nsh = K_SHORT - 1
    for t in range(ts):
        for cb in range(D_QKV // LANES):
            sl = slice(cb * LANES, (cb + 1) * LANES)
            acc = None
            for i in range(K_SHORT):
                j = t + i
                row = qbuf_ref[j, :, sl] if j < nsh else raw_ref[j - nsh, :, sl]
                term = wsh_ref[i:i + 1, sl] * row
                acc = term if acc is None else acc + term
            qkv_ref[t, :, sl] = _qkv_post(acc, cb).astype(BF16)

    gz_ref[...] = _silu(_dot(h, w_ref[:, OFF_Z:OFF_Z + D_MODEL])).astype(BF16).reshape(ts, nb, D_MODEL)
    sgd_ref[...] = _sigmoid(_dot(h, w_ref[:, OFF_GD:OFF_GD + D_MODEL])).astype(BF16).reshape(
        ts, nb, D_MODEL)
    bg_ref[...] = _beta_g_rows(_dot(h, wba_ref[...]), alr_ref[...], dtr_ref[...]).reshape(ts, nb, LANES)


def _mixer_in_dec(xt, cbuft, qbuft, p, *, nb):
    ts, B, _ = xt.shape
    blk = lambda r, n: pl.BlockSpec((r, nb, n), lambda i: (0, i, 0))
    in_specs = [
        blk(ts, D_MODEL), blk(K_CONV - 1, D_MODEL), blk(K_SHORT - 1, D_QKV),
        _resident((1, D_MODEL)),
        _resident(p["wmain"].shape),
        _resident(p["wba"].shape),
        _resident(p["wdw"].shape),
        _resident((1, D_MODEL)),
        _resident((1, D_MODEL)),
        _resident((1, D_MODEL)),
        _resident(p["wcout"].shape),
        _resident(p["wsh"].shape),
        _resident((1, LANES)),
        _resident((1, LANES)),
    ]
    out_shape = [
        jax.ShapeDtypeStruct((ts, B, D_MODEL), BF16),
        jax.ShapeDtypeStruct((ts, B, D_QKV), BF16),
        jax.ShapeDtypeStruct((ts, B, D_MODEL), BF16),
        jax.ShapeDtypeStruct((ts, B, D_MODEL), BF16),
        jax.ShapeDtypeStruct((ts, B, LANES), F32),
        jax.ShapeDtypeStruct((ts, B, D_MODEL), F32),
        jax.ShapeDtypeStruct((ts, B, D_QKV), F32),
    ]
    out_specs = [blk(ts, D_MODEL), blk(ts, D_QKV), blk(ts, D_MODEL), blk(ts, D_MODEL),
                 blk(ts, LANES), blk(ts, D_MODEL), blk(ts, D_QKV)]
    return pl.pallas_call(
        functools.partial(_mixer_in_dec_kernel, ts=ts, nb=nb),
        grid=(B // nb,),
        in_specs=in_specs,
        out_specs=out_specs,
        out_shape=out_shape,
        scratch_shapes=[pltpu.VMEM((ts * nb, D_MODEL), BF16)],
        compiler_params=pltpu.CompilerParams(
            dimension_semantics=("arbitrary",), vmem_limit_bytes=VMEM_LIMIT),
        name="mixer_in_dec",
    )(xt, cbuft, qbuft, p["gmix"], p["wmain"], p["wba"], p["wdw"], p["bdw"], p["lng"], p["lnb"],
      p["wcout"], p["wsh"], p["alog_row"], p["dt_row"])


def _unit_lower_inverse(lm, eye, n):
    x = eye - lm
    pw = lm
    k = 2
    while k < n:
        pb = pw.astype(BF16)
        pw = _dot(pb, pb)
        x = x + _dot(x.astype(BF16), pw.astype(BF16))
        k *= 2
    return x


def _delta_kernel(qkv_ref, gz_ref, bg_ref, bgt_ref, s0_ref, ghead_ref, og_ref, sout_ref, s_ref,
                  *, L, cps):
    c = pl.program_id(1)
    nc = pl.num_programs(1)

    @pl.when(c == 0)
    def _():
        s_ref[...] = s0_ref[0]

    row = lax.broadcasted_iota(jnp.int32, (L, L), 0)
    col = lax.broadcasted_iota(jnp.int32, (L, L), 1)
    causal = row >= col
    strict = row > col
    tril = jnp.where(causal, 1.0, 0.0).astype(BF16)
    triu = jnp.where(row <= col, 1.0, 0.0).astype(BF16)
    eye = jnp.where(row == col, 1.0, 0.0).astype(F32)
    ghead = ghead_ref[...]

    for ci in range(cps):
        r0 = ci * L
        bgc = bg_ref[0, r0:r0 + L, :]
        bgr = bgt_ref[0, :, r0:r0 + L]
        c1, c2, c3 = _split3(bgc)
        gcum_c = _dot(tril, c1) + _dot(tril, c2) + _dot(tril, c3)
        r1, r2, r3 = _split3(bgr)
        gcum_r = _dot(r1, triu) + _dot(r2, triu) + _dot(r3, triu)
        for hd in range(N_HEADS):
            hs = slice(hd * D_HEAD, (hd + 1) * D_HEAD)
            q = qkv_ref[0, r0:r0 + L, hd * D_HEAD:(hd + 1) * D_HEAD]
            k = qkv_ref[0, r0:r0 + L, D_MODEL + hd * D_HEAD:D_MODEL + (hd + 1) * D_HEAD]
            v = qkv_ref[0, r0:r0 + L, 2 * D_MODEL + hd * D_HEAD:2 * D_MODEL + (hd + 1) * D_HEAD]
            qf, kf, vf = q.astype(F32), k.astype(F32), v.astype(F32)
            beta = bgc[:, hd:hd + 1]
            gcol = gcum_c[:, N_HEADS + hd:N_HEADS + hd + 1]
            grow = gcum_r[N_HEADS + hd:N_HEADS + hd + 1, :]
            glast = gcum_c[L - 1:L, N_HEADS + hd:N_HEADS + hd + 1]
            diff = jnp.where(causal, gcol - grow, 0.0)
            decay = jnp.where(causal, jnp.exp(diff), 0.0)
            lm = jnp.where(strict, beta * _dot_nt(k, k) * decay, 0.0)
            tinv = _unit_lower_inverse(lm, eye, L)
            eg = jnp.exp(gcol)
            rhs = jnp.concatenate([vf * beta, kf * (beta * eg)], axis=1)
            sol = _dot(tinv.astype(BF16), rhs.astype(BF16))
            value, kcum = sol[:, :D_HEAD], sol[:, D_HEAD:]
            attn = _dot_nt(q, k) * decay
            q_dec = qf * eg
            k_dec = kf * jnp.exp(glast - gcol)
            s = s_ref[hd]
            sb = s.astype(BF16)
            u = value - _dot(kcum.astype(BF16), sb)
            ub = u.astype(BF16)
            o = _dot(q_dec.astype(BF16), sb) + _dot(attn.astype(BF16), ub)
            s_ref[hd] = s * jnp.exp(glast) + _dot_tn(k_dec.astype(BF16), ub)
            on = _rms(o, ghead) * gz_ref[0, r0:r0 + L, hs].astype(F32)
            og_ref[0, r0:r0 + L, hs] = on.astype(BF16)

    @pl.when(c == nc - 1)
    def _():
        sout_ref[0] = s_ref[...]


def _delta(qkv, gz, bg, bgt, s0, ghead, *, L, cps):
    B, T, _ = qkv.shape
    tb = L * cps
    bs = s0.shape[0]
    s_idx = (lambda b, c: (b, 0, 0, 0)) if bs == B else (lambda b, c: (0, 0, 0, 0))
    tile = lambda n: pl.BlockSpec((1, tb, n), lambda b, c: (b, c, 0))
    return pl.pallas_call(
        functools.partial(_delta_kernel, L=L, cps=cps),
        grid=(B, T // tb),
        in_specs=[
            tile(D_QKV), tile(D_MODEL), tile(LANES),
            pl.BlockSpec((1, 2 * N_HEADS, tb), lambda b, c: (b, 0, c)),
            pl.BlockSpec((1, N_HEADS, D_HEAD, D_HEAD), s_idx),
            _resident((1, D_HEAD)),
        ],
        out_specs=[
            tile(D_MODEL),
            pl.BlockSpec((1, N_HEADS, D_HEAD, D_HEAD), lambda b, c: (b, 0, 0, 0)),
        ],
        out_shape=[
            jax.ShapeDtypeStruct((B, T, D_MODEL), BF16),
            jax.ShapeDtypeStruct((B, N_HEADS, D_HEAD, D_HEAD), F32),
        ],
        scratch_shapes=[pltpu.VMEM((N_HEADS, D_HEAD, D_HEAD), F32)],
        compiler_params=pltpu.CompilerParams(
            dimension_semantics=("arbitrary", "arbitrary"), vmem_limit_bytes=VMEM_LIMIT),
        name="delta_rule",
    )(qkv, gz, bg, bgt, s0, ghead)


def _mixer_out_kernel(x_ref, a_ref, og_ref, sgd_ref, wdout_ref, wo_ref, gmlp_ref, wup_ref, wdown_ref,
                      gfin_ref, y_ref):
    dn = _dot(og_ref[...], wdout_ref[...])
    mix = a_ref[...].astype(F32) + sgd_ref[...].astype(F32) * dn
    x2 = x_ref[...] + _dot(mix.astype(BF16), wo_ref[...])
    hn = _rms(x2, gmlp_ref[...]).astype(BF16)
    acc = x2
    for j in range(0, D_FF, D_MODEL):
        hm = jnp.maximum(_dot(hn, wup_ref[:, j:j + D_MODEL]), 0.0)
        acc = acc + _dot((hm * hm).astype(BF16), wdown_ref[j:j + D_MODEL, :])
    y_ref[...] = _rms(acc, gfin_ref[...])


def _mixer_out(x, a, og, sgd, p, *, tm):
    n = x.shape[0]
    tile = pl.BlockSpec((tm, D_MODEL), lambda i: (i, 0))
    return pl.pallas_call(
        _mixer_out_kernel,
        grid=(n // tm,),
        in_specs=[tile, tile, tile, tile,
                  _resident(p["wdout"].shape), _resident(p["wo"].shape), _resident((1, D_MODEL)),
                  _resident(p["wup"].shape), _resident(p["wdown"].shape), _resident((1, D_MODEL))],
        out_specs=tile,
        out_shape=jax.ShapeDtypeStruct((n, D_MODEL), F32),
        compiler_params=pltpu.CompilerParams(
            dimension_semantics=("arbitrary",), vmem_limit_bytes=VMEM_LIMIT),
        name="mixer_out",
    )(x, a, og, sgd, p["wdout"], p["wo"], p["gmlp"], p["wup"], p["wdown"], p["gfin"])


def _prep_params(g_mix, w_in, w_dw, b_dw, ln_g, ln_b, w_cout, w_short, a_log, dt_bias, g_head,
                 w_dout, w_o, g_mlp, w_up, w_down, g_final):
    w = w_in[0]
    n_main = 2 * D_MODEL + D_QKV + D_MODEL
    ba = w[:, n_main:n_main + 2 * N_HEADS]
    pad16 = lambda v: jnp.pad(v.astype(F32), (N_HEADS, 0))
    return dict(
        wmain=jnp.concatenate([w[:, :n_main], w[:, n_main + 2 * N_HEADS:]], axis=1).astype(BF16),
        wba=jnp.pad(ba, ((0, 0), (0, LANES - 2 * N_HEADS))).astype(BF16),
        wbat=ba.T.astype(BF16),
        gmix=g_mix[0].reshape(1, D_MODEL),
        wdw=w_dw[0], bdw=b_dw[0].reshape(1, D_MODEL),
        lng=ln_g[0].reshape(1, D_MODEL), lnb=ln_b[0].reshape(1, D_MODEL),
        wcout=w_cout[0].astype(BF16), wsh=w_short[0],
        alog_row=jnp.pad(pad16(a_log[0]), (0, LANES - 2 * N_HEADS)).reshape(1, LANES),
        dt_row=jnp.pad(pad16(dt_bias[0]), (0, LANES - 2 * N_HEADS)).reshape(1, LANES),
        alog_col=pad16(a_log[0]).reshape(2 * N_HEADS, 1),
        dt_col=pad16(dt_bias[0]).reshape(2 * N_HEADS, 1),
        ghead=g_head[0].reshape(1, D_HEAD),
        wdout=w_dout[0].astype(BF16), wo=w_o[0].astype(BF16), gmlp=g_mlp[0].reshape(1, D_MODEL),
        wup=w_up[0].astype(BF16), wdown=w_down[0].astype(BF16), gfin=g_final.reshape(1, D_MODEL),
    )


DEC_PAD = 16


def kernel(x_prompt, x_sample, state_conv, state_qkv_conv, state_delta, meta_tokens, g_mix, w_in, w_dw, b_dw, ln_g, ln_b, w_cout, w_short, a_log, dt_bias, g_head, w_dout, w_o, g_mlp, w_up, w_down, g_final):
    p = _prep_params(g_mix, w_in, w_dw, b_dw, ln_g, ln_b, w_cout, w_short, a_log, dt_bias, g_head,
                     w_dout, w_o, g_mlp, w_up, w_down, g_final)
    bp, seq, _ = x_prompt.shape
    bd, ts, _ = x_sample.shape

    zc = jnp.zeros((1, CONV_HALO, D_MODEL), F32)
    zq = jnp.zeros((1, SHORT_HALO, D_QKV), F32)
    zs = jnp.zeros((1, N_HEADS, D_HEAD, D_HEAD), F32)
    xm = meta_tokens.astype(F32)[None]
    _, qkv_m, gz_m, _, bg_m, bgt_m, cv_m, qk_m = _mixer_in_seq(xm, zc, zq, p, tm=N_META, row_block=N_META)
    _, s_m = _delta(qkv_m, gz_m, bg_m, bgt_m, zs, p["ghead"], L=N_META, cps=1)

    a_p, qkv_p, gz_p, sgd_p, bg_p, bgt_p, cv_p, qk_p = _mixer_in_seq(
        x_prompt, cv_m, qk_m, p, tm=256, row_block=32)
    og_p, s_p = _delta(qkv_p, gz_p, bg_p, bgt_p, s_m, p["ghead"], L=64, cps=4)
    n_p = bp * seq
    y_prompt = _mixer_out(x_prompt.reshape(n_p, D_MODEL), a_p.reshape(n_p, D_MODEL),
                          og_p.reshape(n_p, D_MODEL), sgd_p.reshape(n_p, D_MODEL), p,
                          tm=512).reshape(bp, seq, D_MODEL)

    tmaj = lambda v: jnp.transpose(v, (1, 0, 2))
    a_s, qkv_s, gz_s, sgd_s, bg_s, glu_s, raw_s = _mixer_in_dec(
        tmaj(x_sample), tmaj(state_conv[0]), tmaj(state_qkv_conv[0]), p, nb=32)
    padt = lambda v: jnp.pad(tmaj(v), ((0, 0), (0, DEC_PAD - ts), (0, 0)))
    bgt_s = jnp.pad(jnp.transpose(bg_s[:, :, :2 * N_HEADS], (1, 2, 0)), ((0, 0), (0, 0), (0, DEC_PAD - ts)))
    og_s, s_s = _delta(padt(qkv_s), padt(gz_s), padt(bg_s), bgt_s, state_delta[0], p["ghead"],
                       L=DEC_PAD, cps=1)
    n_s = bd * ts
    y_sample = _mixer_out(x_sample.reshape(n_s, D_MODEL), tmaj(a_s).reshape(n_s, D_MODEL),
                          og_s[:, :ts].reshape(n_s, D_MODEL), tmaj(sgd_s).reshape(n_s, D_MODEL), p,
                          tm=256).reshape(bd, ts, D_MODEL)

    nbuf, nsh = K_CONV - 1, K_SHORT - 1
    prompt_conv = cv_p[:, CONV_HALO - nbuf:][None]
    prompt_qkv_conv = qk_p[:, SHORT_HALO - nsh:][None]
    sample_conv = jnp.concatenate([state_conv[0][:, ts:], tmaj(glu_s)], axis=1)[None]
    sample_qkv_conv = tmaj(raw_s[ts - nsh:])[None]
    return (y_prompt, y_sample, prompt_conv, prompt_qkv_conv, s_p[None],
            sample_conv, sample_qkv_conv, s_s[None])
```

```python
import functools

import jax
import jax.numpy as jnp
from jax import lax
from jax.experimental import pallas as pl
from jax.experimental.pallas import tpu as pltpu

F32 = jnp.float32
BF16 = jnp.bfloat16

D_MODEL = 1024
N_HEADS = 8
D_HEAD = 128
D_QKV = 3 * D_MODEL
D_FF = 4 * D_MODEL
K_CONV = 31
K_SHORT = 4
N_META = 16
EPS = 1e-6

CONV_HALO = 32
SHORT_HALO = 8
LANES = 128
SUBLANES = 8
COL_BLOCK = 512
LN_ROWS = 16

OFF_GLU_A, OFF_GLU_B, OFF_QKV, OFF_Z, OFF_GC, OFF_GD = 0, 1024, 2048, 5120, 6144, 7168

VMEM_LIMIT = 56 * 1024 * 1024


def _dot(a, b):
    return jnp.dot(a, b, preferred_element_type=F32)


def _dot_nt(a, b):
    return lax.dot_general(a, b, (((1,), (1,)), ((), ())), preferred_element_type=F32)


def _dot_tn(a, b):
    return lax.dot_general(a, b, (((0,), (0,)), ((), ())), preferred_element_type=F32)


def _sigmoid(x):
    return 0.5 * jnp.tanh(0.5 * x) + 0.5


def _silu(x):
    hx = 0.5 * x
    return hx * jnp.tanh(hx) + hx


def _softplus(x):
    return jnp.maximum(x, 0.0) + jnp.log(1.0 + jnp.exp(-jnp.abs(x)))


def _rms(x, g):
    return x * lax.rsqrt(jnp.mean(x * x, axis=-1, keepdims=True) + EPS) * g


def _layer_norm_silu(c, g, b):
    mu = jnp.mean(c, axis=-1, keepdims=True)
    d = c - mu
    var = jnp.mean(d * d, axis=-1, keepdims=True)
    return _silu(d * lax.rsqrt(var + EPS) * g + b)


def _split3(x):
    x1 = x.astype(BF16)
    r1 = x - x1.astype(F32)
    x2 = r1.astype(BF16)
    x3 = (r1 - x2.astype(F32)).astype(BF16)
    return x1, x2, x3


def _resident(shape):
    nd = len(shape)
    return pl.BlockSpec(shape, lambda *_: (0,) * nd, pipeline_mode=pl.Buffered(1))


def _beta_g_rows(raw, alog_row, dt_row):
    lane = lax.broadcasted_iota(jnp.int32, raw.shape, 1)
    g = -jnp.exp(alog_row) * _softplus(raw + dt_row)
    return jnp.where(lane < N_HEADS, _sigmoid(raw), g)


def _qkv_post(acc, cb):
    y = _silu(acc)
    if cb < 2 * N_HEADS:
        y = y * lax.rsqrt(jnp.sum(y * y, axis=-1, keepdims=True) + 1e-6)
        if cb < N_HEADS:
            y = y * (D_HEAD ** -0.5)
    return y


def _mixer_in_seq_kernel(x_ref, cbuf_ref, qbuf_ref, gmix_ref, w_ref, wba_ref, wbat_ref, wdw_ref,
                         bdw_ref, lng_ref, lnb_ref, wcout_ref, wsh_ref, alr_ref, dtr_ref, alc_ref,
                         dtc_ref, a_ref, qkv_ref, gz_ref, sgd_ref, bg_ref, bgt_ref, ncv_ref, nqk_ref,
                         ext_ref, qext_ref, c_ref, y_ref, *, tm, row_block):
    t = pl.program_id(1)
    nt = pl.num_programs(1)
    nsh = K_SHORT - 1

    @pl.when(t == 0)
    def _():
        for r in range(SUBLANES):
            ext_ref[r, 0:CONV_HALO - r, :] = cbuf_ref[0, r:CONV_HALO, :]
        for s in range(nsh):
            qext_ref[s, SHORT_HALO:SHORT_HALO + nsh - s, :] = qbuf_ref[0, SHORT_HALO - nsh + s:SHORT_HALO, :]

    if tm >= CONV_HALO:
        @pl.when(t > 0)
        def _():
            for r in range(SUBLANES):
                ext_ref[r, 0:CONV_HALO - r, :] = ext_ref[r, tm:tm + CONV_HALO - r, :]
            for s in range(nsh):
                qext_ref[s, SHORT_HALO:SHORT_HALO + nsh - s, :] = qext_ref[
                    s, tm + SHORT_HALO:tm + SHORT_HALO + nsh - s, :]

    x = x_ref[0]
    h = _rms(x, gmix_ref[...]).astype(BF16)

    for j in range(0, D_MODEL, COL_BLOCK):
        a = _dot(h, w_ref[:, OFF_GLU_A + j:OFF_GLU_A + j + COL_BLOCK])
        b = _dot(h, w_ref[:, OFF_GLU_B + j:OFF_GLU_B + j + COL_BLOCK])
        glu = a * _sigmoid(b)
        for r in range(SUBLANES):
            ext_ref[r, CONV_HALO - r:CONV_HALO - r + tm, j:j + COL_BLOCK] = glu

    def qkv_cols(j):
        raw = _dot(h, w_ref[:, OFF_QKV + j:OFF_QKV + j + COL_BLOCK])
        for s in range(K_SHORT):
            lo = SHORT_HALO + nsh - s
            qext_ref[s, lo:lo + tm, j:j + COL_BLOCK] = raw

    def short_conv(cb):
        sl = slice(cb * LANES, (cb + 1) * LANES)
        acc = wsh_ref[0:1, sl] * qext_ref[0, SHORT_HALO:SHORT_HALO + tm, sl]
        for i in range(1, K_SHORT):
            acc = acc + wsh_ref[i:i + 1, sl] * qext_ref[i, SHORT_HALO:SHORT_HALO + tm, sl]
        qkv_ref[0, :, sl] = _qkv_post(acc, cb).astype(BF16)

    def gated(out_ref, off, act, j):
        cs = slice(j, j + COL_BLOCK)
        out_ref[0, :, cs] = act(_dot(h, w_ref[:, off + j:off + j + COL_BLOCK])).astype(BF16)

    def beta_g():
        bg_ref[0] = _beta_g_rows(_dot(h, wba_ref[...]), alr_ref[...], dtr_ref[...])
        rawt = _dot_nt(wbat_ref[...], h)
        srow = lax.broadcasted_iota(jnp.int32, rawt.shape, 0)
        gt = -jnp.exp(alc_ref[...]) * _softplus(rawt + dtc_ref[...])
        bgt_ref[0] = jnp.where(srow < N_HEADS, _sigmoid(rawt), gt)

    side = []
    for j in range(0, D_QKV, COL_BLOCK):
        side.append(functools.partial(qkv_cols, j))
        side += [functools.partial(short_conv, cb)
                 for cb in range(j // LANES, (j + COL_BLOCK) // LANES)]
    for j in range(0, D_MODEL, COL_BLOCK):
        side.append(functools.partial(gated, gz_ref, OFF_Z, _silu, j))
        side.append(functools.partial(gated, sgd_ref, OFF_GD, _sigmoid, j))
    side.append(beta_g)

    first = CONV_HALO - (K_CONV - 1)
    ng = row_block // SUBLANES
    conv_pieces = [(rb, lb) for rb in range(0, tm, row_block) for lb in range(0, D_MODEL, LANES)]
    n_side, emitted = len(side), 0
    for idx, (rb, lb) in enumerate(conv_pieces):
        ls = slice(lb, lb + LANES)
        acc = None
        for i in range(K_CONV):
            blk, r = divmod(first + i, SUBLANES)
            lo = rb + blk * SUBLANES
            term = wdw_ref[i, :, ls][None] * ext_ref[r, lo:lo + row_block, ls].reshape(
                ng, SUBLANES, LANES)
            acc = term if acc is None else acc + term
        y_ref[rb:rb + row_block, ls] = acc.reshape(row_block, LANES)
        while emitted * len(conv_pieces) < (idx + 1) * n_side:
            side[emitted]()
            emitted += 1
    for rb in range(0, tm, LN_ROWS):
        y = y_ref[rb:rb + LN_ROWS, :] + bdw_ref[...]
        c_ref[rb:rb + LN_ROWS, :] = _layer_norm_silu(y, lng_ref[...], lnb_ref[...]).astype(BF16)

    for j in range(0, D_MODEL, COL_BLOCK):
        cs = slice(j, j + COL_BLOCK)
        conv_out = _dot(c_ref[...], wcout_ref[:, cs])
        gate_c = _dot(h, w_ref[:, OFF_GC + j:OFF_GC + j + COL_BLOCK])
        a_ref[0, :, cs] = (_sigmoid(gate_c) * conv_out).astype(BF16)

    @pl.when(t == nt - 1)
    def _():
        ncv_ref[0] = ext_ref[0, tm:tm + CONV_HALO, :]
        nqk_ref[0] = qext_ref[nsh, tm:tm + SHORT_HALO, :]


def _mixer_in_seq(x, cbuf, qbuf, p, *, tm, row_block):
    B, T, _ = x.shape
    nt = T // tm
    bs = cbuf.shape[0]
    state_idx = (lambda b, t: (b, 0, 0)) if bs == B else (lambda b, t: (0, 0, 0))
    tile = lambda n: pl.BlockSpec((1, tm, n), lambda b, t: (b, t, 0))
    in_specs = [
        tile(D_MODEL),
        pl.BlockSpec((1, CONV_HALO, D_MODEL), state_idx),
        pl.BlockSpec((1, SHORT_HALO, D_QKV), state_idx),
        _resident((1, D_MODEL)),
        _resident(p["wmain"].shape),
        _resident(p["wba"].shape),
        _resident(p["wbat"].shape),
        _resident(p["wdw8"].shape),
        _resident((1, D_MODEL)),
        _resident((1, D_MODEL)),
        _resident((1, D_MODEL)),
        _resident(p["wcout"].shape),
        _resident(p["wsh"].shape),
        _resident((1, LANES)),
        _resident((1, LANES)),
        _resident((2 * N_HEADS, 1)),
        _resident((2 * N_HEADS, 1)),
    ]
    out_shape = [
        jax.ShapeDtypeStruct((B, T, D_MODEL), BF16),
        jax.ShapeDtypeStruct((B, T, D_QKV), BF16),
        jax.ShapeDtypeStruct((B, T, D_MODEL), BF16),
        jax.ShapeDtypeStruct((B, T, D_MODEL), BF16),
        jax.ShapeDtypeStruct((B, T, LANES), F32),
        jax.ShapeDtypeStruct((B, 2 * N_HEADS, T), F32),
        jax.ShapeDtypeStruct((B, CONV_HALO, D_MODEL), F32),
        jax.ShapeDtypeStruct((B, SHORT_HALO, D_QKV), F32),
    ]
    out_specs = [
        tile(D_MODEL), tile(D_QKV), tile(D_MODEL), tile(D_MODEL), tile(LANES),
        pl.BlockSpec((1, 2 * N_HEADS, tm), lambda b, t: (b, 0, t)),
        pl.BlockSpec((1, CONV_HALO, D_MODEL), lambda b, t: (b, 0, 0)),
        pl.BlockSpec((1, SHORT_HALO, D_QKV), lambda b, t: (b, 0, 0)),
    ]
    return pl.pallas_call(
        functools.partial(_mixer_in_seq_kernel, tm=tm, row_block=row_block),
        grid=(B, nt),
        in_specs=in_specs,
        out_specs=out_specs,
        out_shape=out_shape,
        scratch_shapes=[
            pltpu.VMEM((SUBLANES, tm + CONV_HALO, D_MODEL), F32),
            pltpu.VMEM((K_SHORT, tm + 2 * SHORT_HALO, D_QKV), F32),
            pltpu.VMEM((tm, D_MODEL), BF16),
            pltpu.VMEM((tm, D_MODEL), F32),
        ],
        compiler_params=pltpu.CompilerParams(
            dimension_semantics=("arbitrary", "arbitrary"), vmem_limit_bytes=VMEM_LIMIT),
        name="mixer_in_seq",
    )(x, cbuf, qbuf, p["gmix"], p["wmain"], p["wba"], p["wbat"], p["wdw8"], p["bdw"], p["lng"],
      p["lnb"], p["wcout"], p["wsh"], p["alog_row"], p["dt_row"], p["alog_col"], p["dt_col"])


def _mixer_in_dec_kernel(x_ref, cbuf_ref, qbuf_ref, gmix_ref, w_ref, wba_ref, wdw_ref, bdw_ref,
                         lng_ref, lnb_ref, wcout_ref, wsh_ref, alr_ref, dtr_ref,
                         a_ref, qkv_ref, gz_ref, sgd_ref, bg_ref, glu_ref, raw_ref, c_ref, *, ts, nb):
    n = ts * nb
    x = x_ref[...].reshape(n, D_MODEL)
    h = _rms(x, gmix_ref[...]).astype(BF16)

    for j in range(0, D_MODEL, COL_BLOCK):
        a = _dot(h, w_ref[:, OFF_GLU_A + j:OFF_GLU_A + j + COL_BLOCK])
        b = _dot(h, w_ref[:, OFF_GLU_B + j:OFF_GLU_B + j + COL_BLOCK])
        glu_ref[:, :, j:j + COL_BLOCK] = (a * _sigmoid(b)).reshape(ts, nb, COL_BLOCK)

    nbuf = K_CONV - 1
    for t in range(ts):
        acc = jnp.zeros((nb, D_MODEL), F32) + bdw_ref[...]
        for i in range(K_CONV):
            j = t + i
            row = cbuf_ref[j] if j < nbuf else glu_ref[j - nbuf]
            acc = acc + wdw_ref[i:i + 1, :] * row
        c_ref[t * nb:(t + 1) * nb, :] = _layer_norm_silu(acc, lng_ref[...], lnb_ref[...]).astype(BF16)

    conv_out = _dot(c_ref[...], wcout_ref[...])
    gate_c = _dot(h, w_ref[:, OFF_GC:OFF_GC + D_MODEL])
    a_ref[...] = (_sigmoid(gate_c) * conv_out).astype(BF16).reshape(ts, nb, D_MODEL)

    for j in range(0, D_QKV, COL_BLOCK):
        raw_ref[:, :, j:j + COL_BLOCK] = _dot(
            h, w_ref[:, OFF_QKV + j:OFF_QKV + j + COL_BLOCK]).reshape(ts, nb, COL_BLOCK)
    nsh = K_SHORT - 1
    for t in range(ts):
        for cb in range(D_QKV // LANES):
            sl = slice(cb * LANES, (cb + 1) * LANES)
            acc = None
            for i in range(K_SHORT):
                j = t + i
                row = qbuf_ref[j, :, sl] if j < nsh else raw_ref[j - nsh, :, sl]
                term = wsh_ref[i:i + 1, sl] * row
                acc = term if acc is None else acc + term
            qkv_ref[t, :, sl] = _qkv_post(acc, cb).astype(BF16)

    gz_ref[...] = _silu(_dot(h, w_ref[:, OFF_Z:OFF_Z + D_MODEL])).astype(BF16).reshape(ts, nb, D_MODEL)
    sgd_ref[...] = _sigmoid(_dot(h, w_ref[:, OFF_GD:OFF_GD + D_MODEL])).astype(BF16).reshape(
        ts, nb, D_MODEL)
    bg_ref[...] = _beta_g_rows(_dot(h, wba_ref[...]), alr_ref[...], dtr_ref[...]).reshape(ts, nb, LANES)


def _mixer_in_dec(xt, cbuft, qbuft, p, *, nb):
    ts, B, _ = xt.shape
    blk = lambda r, n: pl.BlockSpec((r, nb, n), lambda i: (0, i, 0))
    in_specs = [
        blk(ts, D_MODEL), blk(K_CONV - 1, D_MODEL), blk(K_SHORT - 1, D_QKV),
        _resident((1, D_MODEL)),
        _resident(p["wmain"].shape),
        _resident(p["wba"].shape),
        _resident(p["wdw"].shape),
        _resident((1, D_MODEL)),
        _resident((1, D_MODEL)),
        _resident((1, D_MODEL)),
        _resident(p["wcout"].shape),
        _resident(p["wsh"].shape),
        _resident((1, LANES)),
        _resident((1, LANES)),
    ]
    out_shape = [
        jax.ShapeDtypeStruct((ts, B, D_MODEL), BF16),
        jax.ShapeDtypeStruct((ts, B, D_QKV), BF16),
        jax.ShapeDtypeStruct((ts, B, D_MODEL), BF16),
        jax.ShapeDtypeStruct((ts, B, D_MODEL), BF16),
        jax.ShapeDtypeStruct((ts, B, LANES), F32),
        jax.ShapeDtypeStruct((ts, B, D_MODEL), F32),
        jax.ShapeDtypeStruct((ts, B, D_QKV), F32),
    ]
    out_specs = [blk(ts, D_MODEL), blk(ts, D_QKV), blk(ts, D_MODEL), blk(ts, D_MODEL),
                 blk(ts, LANES), blk(ts, D_MODEL), blk(ts, D_QKV)]
    return pl.pallas_call(
        functools.partial(_mixer_in_dec_kernel, ts=ts, nb=nb),
        grid=(B // nb,),
        in_specs=in_specs,
        out_specs=out_specs,
        out_shape=out_shape,
        scratch_shapes=[pltpu.VMEM((ts * nb, D_MODEL), BF16)],
        compiler_params=pltpu.CompilerParams(
            dimension_semantics=("arbitrary",), vmem_limit_bytes=VMEM_LIMIT),
        name="mixer_in_dec",
    )(xt, cbuft, qbuft, p["gmix"], p["wmain"], p["wba"], p["wdw"], p["bdw"], p["lng"], p["lnb"],
      p["wcout"], p["wsh"], p["alog_row"], p["dt_row"])


def _delta_kernel(qkv_ref, gz_ref, bg_ref, bgt_ref, s0_ref, ghead_ref, og_ref, sout_ref, s_ref,
                  *, L, cps, ns):
    c = pl.program_id(1)
    nc = pl.num_programs(1)

    @pl.when(c == 0)
    def _():
        for si in range(ns):
            s_ref[si] = s0_ref[min(si, s0_ref.shape[0] - 1)]

    row = lax.broadcasted_iota(jnp.int32, (L, L), 0)
    col = lax.broadcasted_iota(jnp.int32, (L, L), 1)
    causal = row >= col
    strict = row > col
    tril = jnp.where(causal, 1.0, 0.0).astype(BF16)
    triu = jnp.where(row <= col, 1.0, 0.0).astype(BF16)
    eye = jnp.where(row == col, 1.0, 0.0).astype(F32)
    ghead = ghead_ref[...]

    blocks = [(si, ci) for si in range(ns) for ci in range(cps)]
    items = [(si, ci, hd) for (si, ci) in blocks for hd in range(N_HEADS)]
    each = lambda f: {it: f(*it) for it in items}
    rows = lambda ci: slice(ci * L, (ci + 1) * L)
    head = lambda hd, part=0: slice(part * D_MODEL + hd * D_HEAD, part * D_MODEL + (hd + 1) * D_HEAD)

    bgc, gcum_c, gcum_r = {}, {}, {}
    for si, ci in blocks:
        bgc[si, ci] = bg_ref[si, rows(ci), :]
        c1, c2, c3 = _split3(bgc[si, ci])
        gcum_c[si, ci] = _dot(tril, c1) + _dot(tril, c2) + _dot(tril, c3)
        r1, r2, r3 = _split3(bgt_ref[si, :, rows(ci)])
        gcum_r[si, ci] = _dot(r1, triu) + _dot(r2, triu) + _dot(r3, triu)

    q = each(lambda si, ci, hd: qkv_ref[si, rows(ci), head(hd, 0)])
    k = each(lambda si, ci, hd: qkv_ref[si, rows(ci), head(hd, 1)])
    v = each(lambda si, ci, hd: qkv_ref[si, rows(ci), head(hd, 2)])
    beta = each(lambda si, ci, hd: bgc[si, ci][:, hd:hd + 1])
    gcol = each(lambda si, ci, hd: gcum_c[si, ci][:, N_HEADS + hd:N_HEADS + hd + 1])
    grow = each(lambda si, ci, hd: gcum_r[si, ci][N_HEADS + hd:N_HEADS + hd + 1, :])
    glast = each(lambda si, ci, hd: gcum_c[si, ci][L - 1:L, N_HEADS + hd:N_HEADS + hd + 1])

    decay = each(lambda *it: jnp.where(
        causal, jnp.exp(jnp.where(causal, gcol[it] - grow[it], 0.0)), 0.0))
    kk = each(lambda *it: _dot_nt(k[it], k[it]))
    lm = each(lambda *it: jnp.where(strict, beta[it] * kk[it] * decay[it], 0.0))

    x = each(lambda *it: eye - lm[it])
    pw = lm
    n = 2
    while n < L:
        pb = each(lambda *it: pw[it].astype(BF16))
        pw = each(lambda *it: _dot(pb[it], pb[it]))
        x = each(lambda *it: x[it] + _dot(x[it].astype(BF16), pw[it].astype(BF16)))
        n *= 2

    eg = each(lambda *it: jnp.exp(gcol[it]))
    rhs = each(lambda *it: jnp.concatenate(
        [v[it].astype(F32) * beta[it], k[it].astype(F32) * (beta[it] * eg[it])], axis=1).astype(BF16))
    sol = each(lambda *it: _dot(x[it].astype(BF16), rhs[it]))
    attn = each(lambda *it: (_dot_nt(q[it], k[it]) * decay[it]).astype(BF16))
    q_dec = each(lambda *it: (q[it].astype(F32) * eg[it]).astype(BF16))
    k_dec = each(lambda *it: (k[it].astype(F32) * jnp.exp(glast[it] - gcol[it])).astype(BF16))
    last = each(lambda *it: jnp.exp(glast[it]))

    state = {(si, hd): s_ref[si, hd] for si in range(ns) for hd in range(N_HEADS)}
    for ci in range(cps):
        cur = [(si, ci, hd) for si in range(ns) for hd in range(N_HEADS)]
        now = lambda f: {it: f(*it) for it in cur}
        sb = now(lambda si, ci, hd: state[si, hd].astype(BF16))
        ub = now(lambda *it: (sol[it][:, :D_HEAD] - _dot(sol[it][:, D_HEAD:].astype(BF16), sb[it])
                              ).astype(BF16))
        o = now(lambda *it: _dot(q_dec[it], sb[it]) + _dot(attn[it], ub[it]))
        for it in cur:
            si, _, hd = it
            state[si, hd] = state[si, hd] * last[it] + _dot_tn(k_dec[it], ub[it])
            on = _rms(o[it], ghead) * gz_ref[si, rows(ci), head(hd)].astype(F32)
            og_ref[si, rows(ci), head(hd)] = on.astype(BF16)
    for (si, hd), s in state.items():
        s_ref[si, hd] = s

    @pl.when(c == nc - 1)
    def _():
        sout_ref[...] = s_ref[...]


def _delta(qkv, gz, bg, bgt, s0, ghead, *, L, cps, ns):
    B, T, _ = qkv.shape
    tb = L * cps
    bs = s0.shape[0]
    assert bs == B or (bs == 1 and ns == 1)
    s_spec = (pl.BlockSpec((ns, N_HEADS, D_HEAD, D_HEAD), lambda b, c: (b, 0, 0, 0)) if bs == B else
              pl.BlockSpec((1, N_HEADS, D_HEAD, D_HEAD), lambda b, c: (0, 0, 0, 0)))
    tile = lambda n: pl.BlockSpec((ns, tb, n), lambda b, c: (b, c, 0))
    return pl.pallas_call(
        functools.partial(_delta_kernel, L=L, cps=cps, ns=ns),
        grid=(B // ns, T // tb),
        in_specs=[
            tile(D_QKV), tile(D_MODEL), tile(LANES),
            pl.BlockSpec((ns, 2 * N_HEADS, tb), lambda b, c: (b, 0, c)),
            s_spec,
            _resident((1, D_HEAD)),
        ],
        out_specs=[
            tile(D_MODEL),
            pl.BlockSpec((ns, N_HEADS, D_HEAD, D_HEAD), lambda b, c: (b, 0, 0, 0)),
        ],
        out_shape=[
            jax.ShapeDtypeStruct((B, T, D_MODEL), BF16),
            jax.ShapeDtypeStruct((B, N_HEADS, D_HEAD, D_HEAD), F32),
        ],
        scratch_shapes=[pltpu.VMEM((ns, N_HEADS, D_HEAD, D_HEAD), F32)],
        compiler_params=pltpu.CompilerParams(
            dimension_semantics=("arbitrary", "arbitrary"), vmem_limit_bytes=VMEM_LIMIT),
        name="delta_rule",
    )(qkv, gz, bg, bgt, s0, ghead)


def _mixer_out_kernel(x_ref, a_ref, og_ref, sgd_ref, wdout_ref, wo_ref, gmlp_ref, wup_ref, wdown_ref,
                      gfin_ref, y_ref):
    dn = _dot(og_ref[...], wdout_ref[...])
    mix = a_ref[...].astype(F32) + sgd_ref[...].astype(F32) * dn
    x2 = x_ref[...] + _dot(mix.astype(BF16), wo_ref[...])
    hn = _rms(x2, gmlp_ref[...]).astype(BF16)
    acc = x2
    for j in range(0, D_FF, D_MODEL):
        hm = jnp.maximum(_dot(hn, wup_ref[:, j:j + D_MODEL]), 0.0)
        acc = acc + _dot((hm * hm).astype(BF16), wdown_ref[j:j + D_MODEL, :])
    y_ref[...] = _rms(acc, gfin_ref[...])


def _mixer_out(x, a, og, sgd, p, *, tm):
    n = x.shape[0]
    tile = pl.BlockSpec((tm, D_MODEL), lambda i: (i, 0))
    return pl.pallas_call(
        _mixer_out_kernel,
        grid=(n // tm,),
        in_specs=[tile, tile, tile, tile,
                  _resident(p["wdout"].shape), _resident(p["wo"].shape), _resident((1, D_MODEL)),
                  _resident(p["wup"].shape), _resident(p["wdown"].shape), _resident((1, D_MODEL))],
        out_specs=tile,
        out_shape=jax.ShapeDtypeStruct((n, D_MODEL), F32),
        compiler_params=pltpu.CompilerParams(
            dimension_semantics=("arbitrary",), vmem_limit_bytes=VMEM_LIMIT),
        name="mixer_out",
    )(x, a, og, sgd, p["wdout"], p["wo"], p["gmlp"], p["wup"], p["wdown"], p["gfin"])


def _prep_params(g_mix, w_in, w_dw, b_dw, ln_g, ln_b, w_cout, w_short, a_log, dt_bias, g_head,
                 w_dout, w_o, g_mlp, w_up, w_down, g_final):
    w = w_in[0]
    n_main = 2 * D_MODEL + D_QKV + D_MODEL
    ba = w[:, n_main:n_main + 2 * N_HEADS]
    pad16 = lambda v: jnp.pad(v.astype(F32), (N_HEADS, 0))
    return dict(
        wmain=jnp.concatenate([w[:, :n_main], w[:, n_main + 2 * N_HEADS:]], axis=1).astype(BF16),
        wba=jnp.pad(ba, ((0, 0), (0, LANES - 2 * N_HEADS))).astype(BF16),
        wbat=ba.T.astype(BF16),
        gmix=g_mix[0].reshape(1, D_MODEL),
        wdw=w_dw[0], bdw=b_dw[0].reshape(1, D_MODEL),
        wdw8=jnp.broadcast_to(w_dw[0][:, None, :], (K_CONV, SUBLANES, D_MODEL)),
        lng=ln_g[0].reshape(1, D_MODEL), lnb=ln_b[0].reshape(1, D_MODEL),
        wcout=w_cout[0].astype(BF16), wsh=w_short[0],
        alog_row=jnp.pad(pad16(a_log[0]), (0, LANES - 2 * N_HEADS)).reshape(1, LANES),
        dt_row=jnp.pad(pad16(dt_bias[0]), (0, LANES - 2 * N_HEADS)).reshape(1, LANES),
        alog_col=pad16(a_log[0]).reshape(2 * N_HEADS, 1),
        dt_col=pad16(dt_bias[0]).reshape(2 * N_HEADS, 1),
        ghead=g_head[0].reshape(1, D_HEAD),
        wdout=w_dout[0].astype(BF16), wo=w_o[0].astype(BF16), gmlp=g_mlp[0].reshape(1, D_MODEL),
        wup=w_up[0].astype(BF16), wdown=w_down[0].astype(BF16), gfin=g_final.reshape(1, D_MODEL),
    )


DEC_PAD = 16


def kernel(x_prompt, x_sample, state_conv, state_qkv_conv, state_delta, meta_tokens, g_mix, w_in, w_dw, b_dw, ln_g, ln_b, w_cout, w_short, a_log, dt_bias, g_head, w_dout, w_o, g_mlp, w_up, w_down, g_final):
    p = _prep_params(g_mix, w_in, w_dw, b_dw, ln_g, ln_b, w_cout, w_short, a_log, dt_bias, g_head,
                     w_dout, w_o, g_mlp, w_up, w_down, g_final)
    bp, seq, _ = x_prompt.shape
    bd, ts, _ = x_sample.shape

    zc = jnp.zeros((1, CONV_HALO, D_MODEL), F32)
    zq = jnp.zeros((1, SHORT_HALO, D_QKV), F32)
    zs = jnp.zeros((1, N_HEADS, D_HEAD, D_HEAD), F32)
    xm = meta_tokens.astype(F32)[None]
    _, qkv_m, gz_m, _, bg_m, bgt_m, cv_m, qk_m = _mixer_in_seq(xm, zc, zq, p, tm=N_META, row_block=N_META)
    _, s_m = _delta(qkv_m, gz_m, bg_m, bgt_m, zs, p["ghead"], L=N_META, cps=1, ns=1)

    a_p, qkv_p, gz_p, sgd_p, bg_p, bgt_p, cv_p, qk_p = _mixer_in_seq(
        x_prompt, cv_m, qk_m, p, tm=256, row_block=64)
    og_p, s_p = _delta(qkv_p, gz_p, bg_p, bgt_p, s_m, p["ghead"], L=64, cps=4, ns=1)
    n_p = bp * seq
    y_prompt = _mixer_out(x_prompt.reshape(n_p, D_MODEL), a_p.reshape(n_p, D_MODEL),
                          og_p.reshape(n_p, D_MODEL), sgd_p.reshape(n_p, D_MODEL), p,
                          tm=512).reshape(bp, seq, D_MODEL)

    tmaj = lambda v: jnp.transpose(v, (1, 0, 2))
    a_s, qkv_s, gz_s, sgd_s, bg_s, glu_s, raw_s = _mixer_in_dec(
        tmaj(x_sample), tmaj(state_conv[0]), tmaj(state_qkv_conv[0]), p, nb=32)
    padt = lambda v: jnp.pad(tmaj(v), ((0, 0), (0, DEC_PAD - ts), (0, 0)))
    bgt_s = jnp.pad(jnp.transpose(bg_s[:, :, :2 * N_HEADS], (1, 2, 0)), ((0, 0), (0, 0), (0, DEC_PAD - ts)))
    og_s, s_s = _delta(padt(qkv_s), padt(gz_s), padt(bg_s), bgt_s, state_delta[0], p["ghead"],
                       L=DEC_PAD, cps=1, ns=4)
    n_s = bd * ts
    y_sample = _mixer_out(x_sample.reshape(n_s, D_MODEL), tmaj(a_s).reshape(n_s, D_MODEL),
                          og_s[:, :ts].reshape(n_s, D_MODEL), tmaj(sgd_s).reshape(n_s, D_MODEL), p,
                          tm=256).reshape(bd, ts, D_MODEL)

    nbuf, nsh = K_CONV - 1, K_SHORT - 1
    prompt_conv = cv_p[:, CONV_HALO - nbuf:][None]
    prompt_qkv_conv = qk_p[:, SHORT_HALO - nsh:][None]
    sample_conv = jnp.concatenate([state_conv[0][:, ts:], tmaj(glu_s)], axis=1)[None]
    sample_qkv_conv = tmaj(raw_s[ts - nsh:])[None]
    return (y_prompt, y_sample, prompt_conv, prompt_qkv_conv, s_p[None],
            sample_conv, sample_qkv_conv, s_s[None])
```

```python
import functools

import jax
import jax.numpy as jnp
from jax import lax
from jax.experimental import pallas as pl
from jax.experimental.pallas import tpu as pltpu

F32 = jnp.float32
BF16 = jnp.bfloat16

D_MODEL = 1024
N_HEADS = 8
D_HEAD = 128
D_QKV = 3 * D_MODEL
D_FF = 4 * D_MODEL
K_CONV = 31
K_SHORT = 4
N_META = 16
EPS = 1e-6

CONV_HALO = 32
SHORT_HALO = 8
LANES = 128
SUBLANES = 8
COL_BLOCK = 512
LN_ROWS = 16
ROW_STRIDE = 2
N_SLABS = D_MODEL // LANES
N_QKV_SLABS = D_QKV // LANES

OFF_GLU_A, OFF_GLU_B, OFF_QKV, OFF_Z, OFF_GC, OFF_GD = 0, 1024, 2048, 5120, 6144, 7168

VMEM_LIMIT = 56 * 1024 * 1024

TM_IN_PROMPT = 256
TM_OUT_PROMPT = 512
TM_OUT_DECODE = 256
CONV_ROW_BLOCK = 64
DELTA_CHUNK = 64
DELTA_CHUNKS_PER_STEP = 4
DEC_PAD = 16
DEC_SEQS_PER_STEP = 4
DEC_BATCH_BLOCK = 32


def _dot(a, b):
    return jnp.dot(a, b, preferred_element_type=F32)


def _dot_nt(a, b):
    return lax.dot_general(a, b, (((1,), (1,)), ((), ())), preferred_element_type=F32)


def _dot_tn(a, b):
    return lax.dot_general(a, b, (((0,), (0,)), ((), ())), preferred_element_type=F32)


def _sigmoid(x):
    return 0.5 * jnp.tanh(0.5 * x) + 0.5


def _silu(x):
    hx = 0.5 * x
    return hx * jnp.tanh(hx) + hx


def _softplus(x):
    return jnp.maximum(x, 0.0) + jnp.log(1.0 + jnp.exp(-jnp.abs(x)))


def _rms(x, g):
    return x * lax.rsqrt(jnp.mean(x * x, axis=-1, keepdims=True) + EPS) * g


def _layer_norm_silu(c, g, b):
    mu = jnp.mean(c, axis=-1, keepdims=True)
    d = c - mu
    var = jnp.mean(d * d, axis=-1, keepdims=True)
    return _silu(d * lax.rsqrt(var + EPS) * g + b)


def _split3(x):
    x1 = x.astype(BF16)
    r1 = x - x1.astype(F32)
    x2 = r1.astype(BF16)
    x3 = (r1 - x2.astype(F32)).astype(BF16)
    return x1, x2, x3


def _resident(shape):
    nd = len(shape)
    return pl.BlockSpec(shape, lambda *_: (0,) * nd, pipeline_mode=pl.Buffered(1))


def _beta_g_rows(raw, alog_row, dt_row):
    lane = lax.broadcasted_iota(jnp.int32, raw.shape, 1)
    g = -jnp.exp(alog_row) * _softplus(raw + dt_row)
    return jnp.where(lane < N_HEADS, _sigmoid(raw), g)


def _qkv_post(acc, cb):
    y = _silu(acc)
    if cb < 2 * N_HEADS:
        y = y * lax.rsqrt(jnp.sum(y * y, axis=-1, keepdims=True) + 1e-6)
        if cb < N_HEADS:
            y = y * (D_HEAD ** -0.5)
    return y


def _interleave(main, side):
    done = 0
    for idx, task in enumerate(main):
        task()
        while done * len(main) < (idx + 1) * len(side):
            side[done]()
            done += 1
    for task in side[done:]:
        task()


def _row_groups(lo, hi):
    span = ROW_STRIDE * SUBLANES
    return [lo + m + ph for m in range(0, hi - lo, span) for ph in range(ROW_STRIDE)]


def _strided_conv(src_ref, dst_ref, w_ref, slab, first, n_taps, rows):
    ls = slice(slab * LANES, (slab + 1) * LANES)
    starts = _row_groups(*rows)
    accs = [None] * len(starts)
    for i in range(n_taps):
        w = w_ref[i, :, ls]
        for gi, s in enumerate(starts):
            term = w * src_ref[slab, pl.ds(s + first + i, SUBLANES, stride=ROW_STRIDE), :]
            accs[gi] = term if accs[gi] is None else accs[gi] + term
    for gi, s in enumerate(starts):
        dst_ref[slab, pl.ds(s, SUBLANES, stride=ROW_STRIDE), :] = accs[gi]


def _mixer_in_seq_kernel(x_ref, cbuf_ref, qbuf_ref, gmix_ref, w_ref, wba_ref, wbat_ref, wdw_ref,
                         bdw_ref, lng_ref, lnb_ref, wcout_ref, wsh_ref, alr_ref, dtr_ref, alc_ref,
                         dtc_ref, a_ref, qkv_ref, gz_ref, sgd_ref, bg_ref, bgt_ref, ncv_ref, nqk_ref,
                         ext_ref, yc_ref, qext_ref, qy_ref, c_ref, *, tm, row_block):
    t = pl.program_id(1)
    nt = pl.num_programs(1)
    lanes = lambda k: slice(k * LANES, (k + 1) * LANES)

    @pl.when(t == 0)
    def _():
        for lb in range(N_SLABS):
            ext_ref[lb, 0:CONV_HALO, :] = cbuf_ref[0, :, lanes(lb)]
        for cb in range(N_QKV_SLABS):
            qext_ref[cb, 0:SHORT_HALO, :] = qbuf_ref[0, :, lanes(cb)]

    @pl.when(t > 0)
    def _():
        for lb in range(N_SLABS):
            ext_ref[lb, 0:CONV_HALO, :] = ext_ref[lb, tm:tm + CONV_HALO, :]
        for cb in range(N_QKV_SLABS):
            qext_ref[cb, 0:SHORT_HALO, :] = qext_ref[cb, tm:tm + SHORT_HALO, :]

    x = x_ref[0]
    h = _rms(x, gmix_ref[...]).astype(BF16)

    for j in range(0, D_MODEL, COL_BLOCK):
        a = _dot(h, w_ref[:, OFF_GLU_A + j:OFF_GLU_A + j + COL_BLOCK])
        b = _dot(h, w_ref[:, OFF_GLU_B + j:OFF_GLU_B + j + COL_BLOCK])
        glu = a * _sigmoid(b)
        for c in range(0, COL_BLOCK, LANES):
            ext_ref[(j + c) // LANES, CONV_HALO:CONV_HALO + tm, :] = glu[:, c:c + LANES]

    def qkv_cols(j):
        raw = _dot(h, w_ref[:, OFF_QKV + j:OFF_QKV + j + COL_BLOCK])
        for c in range(0, COL_BLOCK, LANES):
            qext_ref[(j + c) // LANES, SHORT_HALO:SHORT_HALO + tm, :] = raw[:, c:c + LANES]

    def short_conv(cb):
        _strided_conv(qext_ref, qy_ref, wsh_ref, cb, SHORT_HALO - (K_SHORT - 1), K_SHORT, (0, tm))
        qkv_ref[0, :, lanes(cb)] = _qkv_post(qy_ref[cb], cb).astype(BF16)

    def gated(out_ref, off, act, j):
        cs = slice(j, j + COL_BLOCK)
        out_ref[0, :, cs] = act(_dot(h, w_ref[:, off + j:off + j + COL_BLOCK])).astype(BF16)

    def beta_g():
        bg_ref[0] = _beta_g_rows(_dot(h, wba_ref[...]), alr_ref[...], dtr_ref[...])
        rawt = _dot_nt(wbat_ref[...], h)
        srow = lax.broadcasted_iota(jnp.int32, rawt.shape, 0)
        gt = -jnp.exp(alc_ref[...]) * _softplus(rawt + dtc_ref[...])
        bgt_ref[0] = jnp.where(srow < N_HEADS, _sigmoid(rawt), gt)

    side = []
    for j in range(0, D_QKV, COL_BLOCK):
        side.append(functools.partial(qkv_cols, j))
        side += [functools.partial(short_conv, cb)
                 for cb in range(j // LANES, (j + COL_BLOCK) // LANES)]
    for j in range(0, D_MODEL, COL_BLOCK):
        side.append(functools.partial(gated, gz_ref, OFF_Z, _silu, j))
        side.append(functools.partial(gated, sgd_ref, OFF_GD, _sigmoid, j))
    side.append(beta_g)

    conv = [functools.partial(_strided_conv, ext_ref, yc_ref, wdw_ref, lb, CONV_HALO - (K_CONV - 1),
                              K_CONV, (rb, rb + row_block))
            for rb in range(0, tm, row_block) for lb in range(N_SLABS)]
    _interleave(conv, side)
    for rb in range(0, tm, LN_ROWS):
        y = jnp.concatenate([yc_ref[lb, rb:rb + LN_ROWS, :] for lb in range(N_SLABS)], axis=1)
        y = y + bdw_ref[...]
        c_ref[rb:rb + LN_ROWS, :] = _layer_norm_silu(y, lng_ref[...], lnb_ref[...]).astype(BF16)

    for j in range(0, D_MODEL, COL_BLOCK):
        cs = slice(j, j + COL_BLOCK)
        conv_out = _dot(c_ref[...], wcout_ref[:, cs])
        gate_c = _dot(h, w_ref[:, OFF_GC + j:OFF_GC + j + COL_BLOCK])
        a_ref[0, :, cs] = (_sigmoid(gate_c) * conv_out).astype(BF16)

    @pl.when(t == nt - 1)
    def _():
        for lb in range(N_SLABS):
            ncv_ref[0, :, lanes(lb)] = ext_ref[lb, tm:tm + CONV_HALO, :]
        for cb in range(N_QKV_SLABS):
            nqk_ref[0, :, lanes(cb)] = qext_ref[cb, tm:tm + SHORT_HALO, :]


def _mixer_in_seq(x, cbuf, qbuf, p, *, tm, row_block):
    B, T, _ = x.shape
    nt = T // tm
    bs = cbuf.shape[0]
    state_idx = (lambda b, t: (b, 0, 0)) if bs == B else (lambda b, t: (0, 0, 0))
    tile = lambda n: pl.BlockSpec((1, tm, n), lambda b, t: (b, t, 0))
    in_specs = [
        tile(D_MODEL),
        pl.BlockSpec((1, CONV_HALO, D_MODEL), state_idx),
        pl.BlockSpec((1, SHORT_HALO, D_QKV), state_idx),
        _resident((1, D_MODEL)),
        _resident(p["wmain"].shape),
        _resident(p["wba"].shape),
        _resident(p["wbat"].shape),
        _resident(p["wdw8"].shape),
        _resident((1, D_MODEL)),
        _resident((1, D_MODEL)),
        _resident((1, D_MODEL)),
        _resident(p["wcout"].shape),
        _resident(p["wsh8"].shape),
        _resident((1, LANES)),
        _resident((1, LANES)),
        _resident((2 * N_HEADS, 1)),
        _resident((2 * N_HEADS, 1)),
    ]
    out_shape = [
        jax.ShapeDtypeStruct((B, T, D_MODEL), BF16),
        jax.ShapeDtypeStruct((B, T, D_QKV), BF16),
        jax.ShapeDtypeStruct((B, T, D_MODEL), BF16),
        jax.ShapeDtypeStruct((B, T, D_MODEL), BF16),
        jax.ShapeDtypeStruct((B, T, LANES), F32),
        jax.ShapeDtypeStruct((B, 2 * N_HEADS, T), F32),
        jax.ShapeDtypeStruct((B, CONV_HALO, D_MODEL), F32),
        jax.ShapeDtypeStruct((B, SHORT_HALO, D_QKV), F32),
    ]
    out_specs = [
        tile(D_MODEL), tile(D_QKV), tile(D_MODEL), tile(D_MODEL), tile(LANES),
        pl.BlockSpec((1, 2 * N_HEADS, tm), lambda b, t: (b, 0, t)),
        pl.BlockSpec((1, CONV_HALO, D_MODEL), lambda b, t: (b, 0, 0)),
        pl.BlockSpec((1, SHORT_HALO, D_QKV), lambda b, t: (b, 0, 0)),
    ]
    return pl.pallas_call(
        functools.partial(_mixer_in_seq_kernel, tm=tm, row_block=row_block),
        grid=(B, nt),
        in_specs=in_specs,
        out_specs=out_specs,
        out_shape=out_shape,
        scratch_shapes=[
            pltpu.VMEM((N_SLABS, tm + CONV_HALO, LANES), F32),
            pltpu.VMEM((N_SLABS, tm, LANES), F32),
            pltpu.VMEM((N_QKV_SLABS, tm + SHORT_HALO, LANES), F32),
            pltpu.VMEM((N_QKV_SLABS, tm, LANES), F32),
            pltpu.VMEM((tm, D_MODEL), BF16),
        ],
        compiler_params=pltpu.CompilerParams(
            dimension_semantics=("arbitrary", "arbitrary"), vmem_limit_bytes=VMEM_LIMIT),
        name="mixer_in_seq",
    )(x, cbuf, qbuf, p["gmix"], p["wmain"], p["wba"], p["wbat"], p["wdw8"], p["bdw"], p["lng"],
      p["lnb"], p["wcout"], p["wsh8"], p["alog_row"], p["dt_row"], p["alog_col"], p["dt_col"])


def _mixer_in_dec_kernel(x_ref, cbuf_ref, qbuf_ref, gmix_ref, w_ref, wba_ref, wdw_ref, bdw_ref,
                         lng_ref, lnb_ref, wcout_ref, wsh_ref, alr_ref, dtr_ref,
                         a_ref, qkv_ref, gz_ref, sgd_ref, bg_ref, glu_ref, raw_ref, c_ref, *, ts, nb):
    n = ts * nb
    x = x_ref[...].reshape(n, D_MODEL)
    h = _rms(x, gmix_ref[...]).astype(BF16)

    for j in range(0, D_MODEL, COL_BLOCK):
        a = _dot(h, w_ref[:, OFF_GLU_A + j:OFF_GLU_A + j + COL_BLOCK])
        b = _dot(h, w_ref[:, OFF_GLU_B + j:OFF_GLU_B + j + COL_BLOCK])
        glu_ref[:, :, j:j + COL_BLOCK] = (a * _sigmoid(b)).reshape(ts, nb, COL_BLOCK)

    nbuf = K_CONV - 1
    for t in range(ts):
        acc = jnp.zeros((nb, D_MODEL), F32) + bdw_ref[...]
        for i in range(K_CONV):
            j = t + i
            row = cbuf_ref[j] if j < nbuf else glu_ref[j - nbuf]
            acc = acc + wdw_ref[i:i + 1, :] * row
        c_ref[t * nb:(t + 1) * nb, :] = _layer_norm_silu(acc, lng_ref[...], lnb_ref[...]).astype(BF16)

    conv_out = _dot(c_ref[...], wcout_ref[...])
    gate_c = _dot(h, w_ref[:, OFF_GC:OFF_GC + D_MODEL])
    a_ref[...] = (_sigmoid(gate_c) * conv_out).astype(BF16).reshape(ts, nb, D_MODEL)

    for j in range(0, D_QKV, COL_BLOCK):
        raw_ref[:, :, j:j + COL_BLOCK] = _dot(
            h, w_ref[:, OFF_QKV + j:OFF_QKV + j + COL_BLOCK]).reshape(ts, nb, COL_BLOCK)
    nsh = K_SHORT - 1
    for t in range(ts):
        for cb in range(D_QKV // LANES):
            sl = slice(cb * LANES, (cb + 1) * LANES)
            acc = None
            for i in range(K_SHORT):
                j = t + i
                row = qbuf_ref[j, :, sl] if j < nsh else raw_ref[j - nsh, :, sl]
                term = wsh_ref[i:i + 1, sl] * row
                acc = term if acc is None else acc + term
            qkv_ref[t, :, sl] = _qkv_post(acc, cb).astype(BF16)

    gz_ref[...] = _silu(_dot(h, w_ref[:, OFF_Z:OFF_Z + D_MODEL])).astype(BF16).reshape(ts, nb, D_MODEL)
    sgd_ref[...] = _sigmoid(_dot(h, w_ref[:, OFF_GD:OFF_GD + D_MODEL])).astype(BF16).reshape(
        ts, nb, D_MODEL)
    bg_ref[...] = _beta_g_rows(_dot(h, wba_ref[...]), alr_ref[...], dtr_ref[...]).reshape(ts, nb, LANES)


def _mixer_in_dec(xt, cbuft, qbuft, p, *, nb):
    ts, B, _ = xt.shape
    blk = lambda r, n: pl.BlockSpec((r, nb, n), lambda i: (0, i, 0))
    in_specs = [
        blk(ts, D_MODEL), blk(K_CONV - 1, D_MODEL), blk(K_SHORT - 1, D_QKV),
        _resident((1, D_MODEL)),
        _resident(p["wmain"].shape),
        _resident(p["wba"].shape),
        _resident(p["wdw"].shape),
        _resident((1, D_MODEL)),
        _resident((1, D_MODEL)),
        _resident((1, D_MODEL)),
        _resident(p["wcout"].shape),
        _resident(p["wsh"].shape),
        _resident((1, LANES)),
        _resident((1, LANES)),
    ]
    out_shape = [
        jax.ShapeDtypeStruct((ts, B, D_MODEL), BF16),
        jax.ShapeDtypeStruct((ts, B, D_QKV), BF16),
        jax.ShapeDtypeStruct((ts, B, D_MODEL), BF16),
        jax.ShapeDtypeStruct((ts, B, D_MODEL), BF16),
        jax.ShapeDtypeStruct((ts, B, LANES), F32),
        jax.ShapeDtypeStruct((ts, B, D_MODEL), F32),
        jax.ShapeDtypeStruct((ts, B, D_QKV), F32),
    ]
    out_specs = [blk(ts, D_MODEL), blk(ts, D_QKV), blk(ts, D_MODEL), blk(ts, D_MODEL),
                 blk(ts, LANES), blk(ts, D_MODEL), blk(ts, D_QKV)]
    return pl.pallas_call(
        functools.partial(_mixer_in_dec_kernel, ts=ts, nb=nb),
        grid=(B // nb,),
        in_specs=in_specs,
        out_specs=out_specs,
        out_shape=out_shape,
        scratch_shapes=[pltpu.VMEM((ts * nb, D_MODEL), BF16)],
        compiler_params=pltpu.CompilerParams(
            dimension_semantics=("arbitrary",), vmem_limit_bytes=VMEM_LIMIT),
        name="mixer_in_dec",
    )(xt, cbuft, qbuft, p["gmix"], p["wmain"], p["wba"], p["wdw"], p["bdw"], p["lng"], p["lnb"],
      p["wcout"], p["wsh"], p["alog_row"], p["dt_row"])


def _delta_kernel(qkv_ref, gz_ref, bg_ref, bgt_ref, s0_ref, ghead_ref, og_ref, sout_ref, s_ref,
                  *, L, cps, ns):
    c = pl.program_id(1)
    nc = pl.num_programs(1)

    @pl.when(c == 0)
    def _():
        for si in range(ns):
            s_ref[si] = s0_ref[min(si, s0_ref.shape[0] - 1)]

    row = lax.broadcasted_iota(jnp.int32, (L, L), 0)
    col = lax.broadcasted_iota(jnp.int32, (L, L), 1)
    causal = row >= col
    strict = row > col
    tril = jnp.where(causal, 1.0, 0.0).astype(BF16)
    triu = jnp.where(row <= col, 1.0, 0.0).astype(BF16)
    eye = jnp.where(row == col, 1.0, 0.0).astype(F32)
    ghead = ghead_ref[...]

    blocks = [(si, ci) for si in range(ns) for ci in range(cps)]
    items = [(si, ci, hd) for (si, ci) in blocks for hd in range(N_HEADS)]
    each = lambda f: {it: f(*it) for it in items}
    rows = lambda ci: slice(ci * L, (ci + 1) * L)
    head = lambda hd, part=0: slice(part * D_MODEL + hd * D_HEAD, part * D_MODEL + (hd + 1) * D_HEAD)

    bgc, gcum_c, gcum_r = {}, {}, {}
    for si, ci in blocks:
        bgc[si, ci] = bg_ref[si, rows(ci), :]
        c1, c2, c3 = _split3(bgc[si, ci])
        gcum_c[si, ci] = _dot(tril, c1) + _dot(tril, c2) + _dot(tril, c3)
        r1, r2, r3 = _split3(bgt_ref[si, :, rows(ci)])
        gcum_r[si, ci] = _dot(r1, triu) + _dot(r2, triu) + _dot(r3, triu)

    q = each(lambda si, ci, hd: qkv_ref[si, rows(ci), head(hd, 0)])
    k = each(lambda si, ci, hd: qkv_ref[si, rows(ci), head(hd, 1)])
    v = each(lambda si, ci, hd: qkv_ref[si, rows(ci), head(hd, 2)])
    beta = each(lambda si, ci, hd: bgc[si, ci][:, hd:hd + 1])
    gcol = each(lambda si, ci, hd: gcum_c[si, ci][:, N_HEADS + hd:N_HEADS + hd + 1])
    grow = each(lambda si, ci, hd: gcum_r[si, ci][N_HEADS + hd:N_HEADS + hd + 1, :])
    glast = each(lambda si, ci, hd: gcum_c[si, ci][L - 1:L, N_HEADS + hd:N_HEADS + hd + 1])

    decay = each(lambda *it: jnp.where(
        causal, jnp.exp(jnp.where(causal, gcol[it] - grow[it], 0.0)), 0.0))
    kk = each(lambda *it: _dot_nt(k[it], k[it]))
    lm = each(lambda *it: jnp.where(strict, beta[it] * kk[it] * decay[it], 0.0))

    x = each(lambda *it: eye - lm[it])
    pw = lm
    n = 2
    while n < L:
        pb = each(lambda *it: pw[it].astype(BF16))
        pw = each(lambda *it: _dot(pb[it], pb[it]))
        x = each(lambda *it: x[it] + _dot(x[it].astype(BF16), pw[it].astype(BF16)))
        n *= 2

    eg = each(lambda *it: jnp.exp(gcol[it]))
    rhs = each(lambda *it: jnp.concatenate(
        [v[it].astype(F32) * beta[it], k[it].astype(F32) * (beta[it] * eg[it])], axis=1).astype(BF16))
    sol = each(lambda *it: _dot(x[it].astype(BF16), rhs[it]))
    attn = each(lambda *it: (_dot_nt(q[it], k[it]) * decay[it]).astype(BF16))
    q_dec = each(lambda *it: (q[it].astype(F32) * eg[it]).astype(BF16))
    k_dec = each(lambda *it: (k[it].astype(F32) * jnp.exp(glast[it] - gcol[it])).astype(BF16))
    last = each(lambda *it: jnp.exp(glast[it]))

    state = {(si, hd): s_ref[si, hd] for si in range(ns) for hd in range(N_HEADS)}
    for ci in range(cps):
        cur = [(si, ci, hd) for si in range(ns) for hd in range(N_HEADS)]
        now = lambda f: {it: f(*it) for it in cur}
        sb = now(lambda si, ci, hd: state[si, hd].astype(BF16))
        ub = now(lambda *it: (sol[it][:, :D_HEAD] - _dot(sol[it][:, D_HEAD:].astype(BF16), sb[it])
                              ).astype(BF16))
        o = now(lambda *it: _dot(q_dec[it], sb[it]) + _dot(attn[it], ub[it]))
        for it in cur:
            si, _, hd = it
            state[si, hd] = state[si, hd] * last[it] + _dot_tn(k_dec[it], ub[it])
            on = _rms(o[it], ghead) * gz_ref[si, rows(ci), head(hd)].astype(F32)
            og_ref[si, rows(ci), head(hd)] = on.astype(BF16)
    for (si, hd), s in state.items():
        s_ref[si, hd] = s

    @pl.when(c == nc - 1)
    def _():
        sout_ref[...] = s_ref[...]


def _delta(qkv, gz, bg, bgt, s0, ghead, *, L, cps, ns):
    B, T, _ = qkv.shape
    tb = L * cps
    bs = s0.shape[0]
    assert bs == B or (bs == 1 and ns == 1)
    s_spec = (pl.BlockSpec((ns, N_HEADS, D_HEAD, D_HEAD), lambda b, c: (b, 0, 0, 0)) if bs == B else
              pl.BlockSpec((1, N_HEADS, D_HEAD, D_HEAD), lambda b, c: (0, 0, 0, 0)))
    tile = lambda n: pl.BlockSpec((ns, tb, n), lambda b, c: (b, c, 0))
    return pl.pallas_call(
        functools.partial(_delta_kernel, L=L, cps=cps, ns=ns),
        grid=(B // ns, T // tb),
        in_specs=[
            tile(D_QKV), tile(D_MODEL), tile(LANES),
            pl.BlockSpec((ns, 2 * N_HEADS, tb), lambda b, c: (b, 0, c)),
            s_spec,
            _resident((1, D_HEAD)),
        ],
        out_specs=[
            tile(D_MODEL),
            pl.BlockSpec((ns, N_HEADS, D_HEAD, D_HEAD), lambda b, c: (b, 0, 0, 0)),
        ],
        out_shape=[
            jax.ShapeDtypeStruct((B, T, D_MODEL), BF16),
            jax.ShapeDtypeStruct((B, N_HEADS, D_HEAD, D_HEAD), F32),
        ],
        scratch_shapes=[pltpu.VMEM((ns, N_HEADS, D_HEAD, D_HEAD), F32)],
        compiler_params=pltpu.CompilerParams(
            dimension_semantics=("arbitrary", "arbitrary"), vmem_limit_bytes=VMEM_LIMIT),
        name="delta_rule",
    )(qkv, gz, bg, bgt, s0, ghead)


def _mixer_out_kernel(x_ref, a_ref, og_ref, sgd_ref, wdout_ref, wo_ref, gmlp_ref, wup_ref, wdown_ref,
                      gfin_ref, y_ref):
    dn = _dot(og_ref[...], wdout_ref[...])
    mix = a_ref[...].astype(F32) + sgd_ref[...].astype(F32) * dn
    x2 = x_ref[...] + _dot(mix.astype(BF16), wo_ref[...])
    hn = _rms(x2, gmlp_ref[...]).astype(BF16)
    acc = x2
    for j in range(0, D_FF, D_MODEL):
        hm = jnp.maximum(_dot(hn, wup_ref[:, j:j + D_MODEL]), 0.0)
        acc = acc + _dot((hm * hm).astype(BF16), wdown_ref[j:j + D_MODEL, :])
    y_ref[...] = _rms(acc, gfin_ref[...])


def _mixer_out(x, a, og, sgd, p, *, tm):
    n = x.shape[0]
    tile = pl.BlockSpec((tm, D_MODEL), lambda i: (i, 0))
    return pl.pallas_call(
        _mixer_out_kernel,
        grid=(n // tm,),
        in_specs=[tile, tile, tile, tile,
                  _resident(p["wdout"].shape), _resident(p["wo"].shape), _resident((1, D_MODEL)),
                  _resident(p["wup"].shape), _resident(p["wdown"].shape), _resident((1, D_MODEL))],
        out_specs=tile,
        out_shape=jax.ShapeDtypeStruct((n, D_MODEL), F32),
        compiler_params=pltpu.CompilerParams(
            dimension_semantics=("arbitrary",), vmem_limit_bytes=VMEM_LIMIT),
        name="mixer_out",
    )(x, a, og, sgd, p["wdout"], p["wo"], p["gmlp"], p["wup"], p["wdown"], p["gfin"])


def _prep_params(g_mix, w_in, w_dw, b_dw, ln_g, ln_b, w_cout, w_short, a_log, dt_bias, g_head,
                 w_dout, w_o, g_mlp, w_up, w_down, g_final):
    w = w_in[0]
    n_main = 2 * D_MODEL + D_QKV + D_MODEL
    ba = w[:, n_main:n_main + 2 * N_HEADS]
    pad16 = lambda v: jnp.pad(v.astype(F32), (N_HEADS, 0))
    rows8 = lambda v: jnp.broadcast_to(v[:, None, :], (v.shape[0], SUBLANES, v.shape[1]))
    return dict(
        wmain=jnp.concatenate([w[:, :n_main], w[:, n_main + 2 * N_HEADS:]], axis=1).astype(BF16),
        wba=jnp.pad(ba, ((0, 0), (0, LANES - 2 * N_HEADS))).astype(BF16),
        wbat=ba.T.astype(BF16),
        gmix=g_mix[0].reshape(1, D_MODEL),
        wdw=w_dw[0], wdw8=rows8(w_dw[0]), bdw=b_dw[0].reshape(1, D_MODEL),
        lng=ln_g[0].reshape(1, D_MODEL), lnb=ln_b[0].reshape(1, D_MODEL),
        wcout=w_cout[0].astype(BF16), wsh=w_short[0], wsh8=rows8(w_short[0]),
        alog_row=jnp.pad(pad16(a_log[0]), (0, LANES - 2 * N_HEADS)).reshape(1, LANES),
        dt_row=jnp.pad(pad16(dt_bias[0]), (0, LANES - 2 * N_HEADS)).reshape(1, LANES),
        alog_col=pad16(a_log[0]).reshape(2 * N_HEADS, 1),
        dt_col=pad16(dt_bias[0]).reshape(2 * N_HEADS, 1),
        ghead=g_head[0].reshape(1, D_HEAD),
        wdout=w_dout[0].astype(BF16), wo=w_o[0].astype(BF16), gmlp=g_mlp[0].reshape(1, D_MODEL),
        wup=w_up[0].astype(BF16), wdown=w_down[0].astype(BF16), gfin=g_final.reshape(1, D_MODEL),
    )


def kernel(x_prompt, x_sample, state_conv, state_qkv_conv, state_delta, meta_tokens, g_mix, w_in, w_dw, b_dw, ln_g, ln_b, w_cout, w_short, a_log, dt_bias, g_head, w_dout, w_o, g_mlp, w_up, w_down, g_final):
    p = _prep_params(g_mix, w_in, w_dw, b_dw, ln_g, ln_b, w_cout, w_short, a_log, dt_bias, g_head,
                     w_dout, w_o, g_mlp, w_up, w_down, g_final)
    bp, seq, _ = x_prompt.shape
    bd, ts, _ = x_sample.shape

    zc = jnp.zeros((1, CONV_HALO, D_MODEL), F32)
    zq = jnp.zeros((1, SHORT_HALO, D_QKV), F32)
    zs = jnp.zeros((1, N_HEADS, D_HEAD, D_HEAD), F32)
    xm = meta_tokens.astype(F32)[None]
    _, qkv_m, gz_m, _, bg_m, bgt_m, cv_m, qk_m = _mixer_in_seq(xm, zc, zq, p, tm=N_META, row_block=N_META)
    _, s_m = _delta(qkv_m, gz_m, bg_m, bgt_m, zs, p["ghead"], L=N_META, cps=1, ns=1)

    a_p, qkv_p, gz_p, sgd_p, bg_p, bgt_p, cv_p, qk_p = _mixer_in_seq(
        x_prompt, cv_m, qk_m, p, tm=TM_IN_PROMPT, row_block=CONV_ROW_BLOCK)
    og_p, s_p = _delta(qkv_p, gz_p, bg_p, bgt_p, s_m, p["ghead"], L=DELTA_CHUNK,
                       cps=DELTA_CHUNKS_PER_STEP, ns=1)
    n_p = bp * seq
    flat = lambda v: v.reshape(n_p, D_MODEL)
    y_prompt = _mixer_out(flat(x_prompt), flat(a_p), flat(og_p), flat(sgd_p), p,
                          tm=TM_OUT_PROMPT).reshape(bp, seq, D_MODEL)

    tmaj = lambda v: jnp.transpose(v, (1, 0, 2))
    a_s, qkv_s, gz_s, sgd_s, bg_s, glu_s, raw_s = _mixer_in_dec(
        tmaj(x_sample), tmaj(state_conv[0]), tmaj(state_qkv_conv[0]), p, nb=DEC_BATCH_BLOCK)
    padt = lambda v: jnp.pad(tmaj(v), ((0, 0), (0, DEC_PAD - ts), (0, 0)))
    bgt_s = jnp.pad(jnp.transpose(bg_s[:, :, :2 * N_HEADS], (1, 2, 0)), ((0, 0), (0, 0), (0, DEC_PAD - ts)))
    og_s, s_s = _delta(padt(qkv_s), padt(gz_s), padt(bg_s), bgt_s, state_delta[0], p["ghead"],
                       L=DEC_PAD, cps=1, ns=DEC_SEQS_PER_STEP)
    n_s = bd * ts
    y_sample = _mixer_out(x_sample.reshape(n_s, D_MODEL), tmaj(a_s).reshape(n_s, D_MODEL),
                          og_s[:, :ts].reshape(n_s, D_MODEL), tmaj(sgd_s).reshape(n_s, D_MODEL), p,
                          tm=TM_OUT_DECODE).reshape(bd, ts, D_MODEL)

    nbuf, nsh = K_CONV - 1, K_SHORT - 1
    prompt_conv = cv_p[:, CONV_HALO - nbuf:][None]
    prompt_qkv_conv = qk_p[:, SHORT_HALO - nsh:][None]
    sample_conv = jnp.concatenate([state_conv[0][:, ts:], tmaj(glu_s)], axis=1)[None]
    sample_qkv_conv = tmaj(raw_s[ts - nsh:])[None]
    return (y_prompt, y_sample, prompt_conv, prompt_qkv_conv, s_p[None],
            sample_conv, sample_qkv_conv, s_s[None])
```

```python
import functools

import jax
import jax.numpy as jnp
from jax import lax
from jax.experimental import pallas as pl
from jax.experimental.pallas import tpu as pltpu

F32 = jnp.float32
BF16 = jnp.bfloat16

D_MODEL = 1024
N_HEADS = 8
D_HEAD = 128
D_QKV = 3 * D_MODEL
D_FF = 4 * D_MODEL
K_CONV = 31
K_SHORT = 4
N_META = 16
EPS = 1e-6

CONV_HALO = 32
SHORT_HALO = 8
LANES = 128
SUBLANES = 8
COL_BLOCK = 512
LN_ROWS = 16
ROW_STRIDE = 2
N_SLABS = D_MODEL // LANES
N_QKV_SLABS = D_QKV // LANES

OFF_GLU_A, OFF_GLU_B, OFF_QKV, OFF_Z, OFF_GC, OFF_GD = 0, 1024, 2048, 5120, 6144, 7168

VMEM_LIMIT = 56 * 1024 * 1024

TM_IN_PROMPT = 256
TM_OUT_PROMPT = 512
TM_OUT_DECODE = 256
CONV_ROW_BLOCK = 64
DELTA_CHUNK = 64
DELTA_CHUNKS_PER_STEP = 8
DEC_PAD = 16
DEC_SEQS_PER_STEP = 4
DEC_BATCH_BLOCK = 32


def _dot(a, b):
    return jnp.dot(a, b, preferred_element_type=F32)


def _dot_nt(a, b):
    return lax.dot_general(a, b, (((1,), (1,)), ((), ())), preferred_element_type=F32)


def _dot_tn(a, b):
    return lax.dot_general(a, b, (((0,), (0,)), ((), ())), preferred_element_type=F32)


def _sigmoid(x):
    return 0.5 * jnp.tanh(0.5 * x) + 0.5


def _silu(x):
    hx = 0.5 * x
    return hx * jnp.tanh(hx) + hx


def _softplus(x):
    return jnp.maximum(x, 0.0) + jnp.log(1.0 + jnp.exp(-jnp.abs(x)))


def _rms(x, g):
    return x * lax.rsqrt(jnp.mean(x * x, axis=-1, keepdims=True) + EPS) * g


def _layer_norm_silu(c, g, b):
    mu = jnp.mean(c, axis=-1, keepdims=True)
    d = c - mu
    var = jnp.mean(d * d, axis=-1, keepdims=True)
    return _silu(d * lax.rsqrt(var + EPS) * g + b)


def _split3(x):
    x1 = x.astype(BF16)
    r1 = x - x1.astype(F32)
    x2 = r1.astype(BF16)
    x3 = (r1 - x2.astype(F32)).astype(BF16)
    return x1, x2, x3


def _resident(shape):
    nd = len(shape)
    return pl.BlockSpec(shape, lambda *_: (0,) * nd, pipeline_mode=pl.Buffered(1))


def _beta_g_rows(raw, alog_row, dt_row):
    lane = lax.broadcasted_iota(jnp.int32, raw.shape, 1)
    g = -jnp.exp(alog_row) * _softplus(raw + dt_row)
    return jnp.where(lane < N_HEADS, _sigmoid(raw), g)


def _qkv_post(acc, cb):
    y = _silu(acc)
    if cb < 2 * N_HEADS:
        y = y * lax.rsqrt(jnp.sum(y * y, axis=-1, keepdims=True) + 1e-6)
        if cb < N_HEADS:
            y = y * (D_HEAD ** -0.5)
    return y


def _interleave(main, side):
    done = 0
    for idx, task in enumerate(main):
        task()
        while done * len(main) < (idx + 1) * len(side):
            side[done]()
            done += 1
    for task in side[done:]:
        task()


def _row_groups(lo, hi):
    span = ROW_STRIDE * SUBLANES
    return [lo + m + ph for m in range(0, hi - lo, span) for ph in range(ROW_STRIDE)]


def _strided_conv(src_ref, dst_ref, w_ref, slab, first, n_taps, rows):
    ls = slice(slab * LANES, (slab + 1) * LANES)
    starts = _row_groups(*rows)
    accs = [None] * len(starts)
    for i in range(n_taps):
        w = w_ref[i, :, ls]
        for gi, s in enumerate(starts):
            term = w * src_ref[slab, pl.ds(s + first + i, SUBLANES, stride=ROW_STRIDE), :]
            accs[gi] = term if accs[gi] is None else accs[gi] + term
    for gi, s in enumerate(starts):
        dst_ref[slab, pl.ds(s, SUBLANES, stride=ROW_STRIDE), :] = accs[gi]


def _mixer_in_seq_kernel(x_ref, cbuf_ref, qbuf_ref, gmix_ref, w_ref, wba_ref, wbat_ref, wdw_ref,
                         bdw_ref, lng_ref, lnb_ref, wcout_ref, wsh_ref, alr_ref, dtr_ref, alc_ref,
                         dtc_ref, a_ref, qkv_ref, gz_ref, sgd_ref, bg_ref, bgt_ref, ncv_ref, nqk_ref,
                         ext_ref, yc_ref, qext_ref, qy_ref, c_ref, *, tm, row_block):
    t = pl.program_id(1)
    nt = pl.num_programs(1)
    lanes = lambda k: slice(k * LANES, (k + 1) * LANES)

    @pl.when(t == 0)
    def _():
        for lb in range(N_SLABS):
            ext_ref[lb, 0:CONV_HALO, :] = cbuf_ref[0, :, lanes(lb)]
        for cb in range(N_QKV_SLABS):
            qext_ref[cb, 0:SHORT_HALO, :] = qbuf_ref[0, :, lanes(cb)]

    @pl.when(t > 0)
    def _():
        for lb in range(N_SLABS):
            ext_ref[lb, 0:CONV_HALO, :] = ext_ref[lb, tm:tm + CONV_HALO, :]
        for cb in range(N_QKV_SLABS):
            qext_ref[cb, 0:SHORT_HALO, :] = qext_ref[cb, tm:tm + SHORT_HALO, :]

    x = x_ref[0]
    h = _rms(x, gmix_ref[...]).astype(BF16)

    for j in range(0, D_MODEL, COL_BLOCK):
        a = _dot(h, w_ref[:, OFF_GLU_A + j:OFF_GLU_A + j + COL_BLOCK])
        b = _dot(h, w_ref[:, OFF_GLU_B + j:OFF_GLU_B + j + COL_BLOCK])
        glu = a * _sigmoid(b)
        for c in range(0, COL_BLOCK, LANES):
            ext_ref[(j + c) // LANES, CONV_HALO:CONV_HALO + tm, :] = glu[:, c:c + LANES]

    def qkv_cols(j):
        raw = _dot(h, w_ref[:, OFF_QKV + j:OFF_QKV + j + COL_BLOCK])
        for c in range(0, COL_BLOCK, LANES):
            qext_ref[(j + c) // LANES, SHORT_HALO:SHORT_HALO + tm, :] = raw[:, c:c + LANES]

    def short_conv(cb):
        _strided_conv(qext_ref, qy_ref, wsh_ref, cb, SHORT_HALO - (K_SHORT - 1), K_SHORT, (0, tm))
        qkv_ref[0, :, lanes(cb)] = _qkv_post(qy_ref[cb], cb).astype(BF16)

    def gated(out_ref, off, act, j):
        cs = slice(j, j + COL_BLOCK)
        out_ref[0, :, cs] = act(_dot(h, w_ref[:, off + j:off + j + COL_BLOCK])).astype(BF16)

    def beta_g():
        bg_ref[0] = _beta_g_rows(_dot(h, wba_ref[...]), alr_ref[...], dtr_ref[...])
        rawt = _dot_nt(wbat_ref[...], h)
        srow = lax.broadcasted_iota(jnp.int32, rawt.shape, 0)
        gt = -jnp.exp(alc_ref[...]) * _softplus(rawt + dtc_ref[...])
        bgt_ref[0] = jnp.where(srow < N_HEADS, _sigmoid(rawt), gt)

    side = []
    for j in range(0, D_QKV, COL_BLOCK):
        side.append(functools.partial(qkv_cols, j))
        side += [functools.partial(short_conv, cb)
                 for cb in range(j // LANES, (j + COL_BLOCK) // LANES)]
    for j in range(0, D_MODEL, COL_BLOCK):
        side.append(functools.partial(gated, gz_ref, OFF_Z, _silu, j))
        side.append(functools.partial(gated, sgd_ref, OFF_GD, _sigmoid, j))
    side.append(beta_g)

    conv = [functools.partial(_strided_conv, ext_ref, yc_ref, wdw_ref, lb, CONV_HALO - (K_CONV - 1),
                              K_CONV, (rb, rb + row_block))
            for rb in range(0, tm, row_block) for lb in range(N_SLABS)]
    _interleave(conv, side)
    for rb in range(0, tm, LN_ROWS):
        y = jnp.concatenate([yc_ref[lb, rb:rb + LN_ROWS, :] for lb in range(N_SLABS)], axis=1)
        y = y + bdw_ref[...]
        c_ref[rb:rb + LN_ROWS, :] = _layer_norm_silu(y, lng_ref[...], lnb_ref[...]).astype(BF16)

    for j in range(0, D_MODEL, COL_BLOCK):
        cs = slice(j, j + COL_BLOCK)
        conv_out = _dot(c_ref[...], wcout_ref[:, cs])
        gate_c = _dot(h, w_ref[:, OFF_GC + j:OFF_GC + j + COL_BLOCK])
        a_ref[0, :, cs] = (_sigmoid(gate_c) * conv_out).astype(BF16)

    @pl.when(t == nt - 1)
    def _():
        for lb in range(N_SLABS):
            ncv_ref[0, :, lanes(lb)] = ext_ref[lb, tm:tm + CONV_HALO, :]
        for cb in range(N_QKV_SLABS):
            nqk_ref[0, :, lanes(cb)] = qext_ref[cb, tm:tm + SHORT_HALO, :]


def _mixer_in_seq(x, cbuf, qbuf, p, *, tm, row_block):
    B, T, _ = x.shape
    nt = T // tm
    bs = cbuf.shape[0]
    state_idx = (lambda b, t: (b, 0, 0)) if bs == B else (lambda b, t: (0, 0, 0))
    tile = lambda n: pl.BlockSpec((1, tm, n), lambda b, t: (b, t, 0))
    in_specs = [
        tile(D_MODEL),
        pl.BlockSpec((1, CONV_HALO, D_MODEL), state_idx),
        pl.BlockSpec((1, SHORT_HALO, D_QKV), state_idx),
        _resident((1, D_MODEL)),
        _resident(p["wmain"].shape),
        _resident(p["wba"].shape),
        _resident(p["wbat"].shape),
        _resident(p["wdw8"].shape),
        _resident((1, D_MODEL)),
        _resident((1, D_MODEL)),
        _resident((1, D_MODEL)),
        _resident(p["wcout"].shape),
        _resident(p["wsh8"].shape),
        _resident((1, LANES)),
        _resident((1, LANES)),
        _resident((2 * N_HEADS, 1)),
        _resident((2 * N_HEADS, 1)),
    ]
    out_shape = [
        jax.ShapeDtypeStruct((B, T, D_MODEL), BF16),
        jax.ShapeDtypeStruct((B, T, D_QKV), BF16),
        jax.ShapeDtypeStruct((B, T, D_MODEL), BF16),
        jax.ShapeDtypeStruct((B, T, D_MODEL), BF16),
        jax.ShapeDtypeStruct((B, T, LANES), F32),
        jax.ShapeDtypeStruct((B, 2 * N_HEADS, T), F32),
        jax.ShapeDtypeStruct((B, CONV_HALO, D_MODEL), F32),
        jax.ShapeDtypeStruct((B, SHORT_HALO, D_QKV), F32),
    ]
    out_specs = [
        tile(D_MODEL), tile(D_QKV), tile(D_MODEL), tile(D_MODEL), tile(LANES),
        pl.BlockSpec((1, 2 * N_HEADS, tm), lambda b, t: (b, 0, t)),
        pl.BlockSpec((1, CONV_HALO, D_MODEL), lambda b, t: (b, 0, 0)),
        pl.BlockSpec((1, SHORT_HALO, D_QKV), lambda b, t: (b, 0, 0)),
    ]
    return pl.pallas_call(
        functools.partial(_mixer_in_seq_kernel, tm=tm, row_block=row_block),
        grid=(B, nt),
        in_specs=in_specs,
        out_specs=out_specs,
        out_shape=out_shape,
        scratch_shapes=[
            pltpu.VMEM((N_SLABS, tm + CONV_HALO, LANES), F32),
            pltpu.VMEM((N_SLABS, tm, LANES), F32),
            pltpu.VMEM((N_QKV_SLABS, tm + SHORT_HALO, LANES), F32),
            pltpu.VMEM((N_QKV_SLABS, tm, LANES), F32),
            pltpu.VMEM((tm, D_MODEL), BF16),
        ],
        compiler_params=pltpu.CompilerParams(
            dimension_semantics=("arbitrary", "arbitrary"), vmem_limit_bytes=VMEM_LIMIT),
        name="mixer_in_seq",
    )(x, cbuf, qbuf, p["gmix"], p["wmain"], p["wba"], p["wbat"], p["wdw8"], p["bdw"], p["lng"],
      p["lnb"], p["wcout"], p["wsh8"], p["alog_row"], p["dt_row"], p["alog_col"], p["dt_col"])


def _mixer_in_dec_kernel(x_ref, cbuf_ref, qbuf_ref, gmix_ref, w_ref, wba_ref, wdw_ref, bdw_ref,
                         lng_ref, lnb_ref, wcout_ref, wsh_ref, alr_ref, dtr_ref,
                         a_ref, qkv_ref, gz_ref, sgd_ref, bg_ref, glu_ref, raw_ref, c_ref, *, ts, nb):
    n = ts * nb
    x = x_ref[...].reshape(n, D_MODEL)
    h = _rms(x, gmix_ref[...]).astype(BF16)

    for j in range(0, D_MODEL, COL_BLOCK):
        a = _dot(h, w_ref[:, OFF_GLU_A + j:OFF_GLU_A + j + COL_BLOCK])
        b = _dot(h, w_ref[:, OFF_GLU_B + j:OFF_GLU_B + j + COL_BLOCK])
        glu_ref[:, :, j:j + COL_BLOCK] = (a * _sigmoid(b)).reshape(ts, nb, COL_BLOCK)

    nbuf = K_CONV - 1
    for t in range(ts):
        acc = jnp.zeros((nb, D_MODEL), F32) + bdw_ref[...]
        for i in range(K_CONV):
            j = t + i
            row = cbuf_ref[j] if j < nbuf else glu_ref[j - nbuf]
            acc = acc + wdw_ref[i:i + 1, :] * row
        c_ref[t * nb:(t + 1) * nb, :] = _layer_norm_silu(acc, lng_ref[...], lnb_ref[...]).astype(BF16)

    conv_out = _dot(c_ref[...], wcout_ref[...])
    gate_c = _dot(h, w_ref[:, OFF_GC:OFF_GC + D_MODEL])
    a_ref[...] = (_sigmoid(gate_c) * conv_out).astype(BF16).reshape(ts, nb, D_MODEL)

    for j in range(0, D_QKV, COL_BLOCK):
        raw_ref[:, :, j:j + COL_BLOCK] = _dot(
            h, w_ref[:, OFF_QKV + j:OFF_QKV + j + COL_BLOCK]).reshape(ts, nb, COL_BLOCK)
    nsh = K_SHORT - 1
    for t in range(ts):
        for cb in range(D_QKV // LANES):
            sl = slice(cb * LANES, (cb + 1) * LANES)
            acc = None
            for i in range(K_SHORT):
                j = t + i
                row = qbuf_ref[j, :, sl] if j < nsh else raw_ref[j - nsh, :, sl]
                term = wsh_ref[i:i + 1, sl] * row
                acc = term if acc is None else acc + term
            qkv_ref[t, :, sl] = _qkv_post(acc, cb).astype(BF16)

    gz_ref[...] = _silu(_dot(h, w_ref[:, OFF_Z:OFF_Z + D_MODEL])).astype(BF16).reshape(ts, nb, D_MODEL)
    sgd_ref[...] = _sigmoid(_dot(h, w_ref[:, OFF_GD:OFF_GD + D_MODEL])).astype(BF16).reshape(
        ts, nb, D_MODEL)
    bg_ref[...] = _beta_g_rows(_dot(h, wba_ref[...]), alr_ref[...], dtr_ref[...]).reshape(ts, nb, LANES)


def _mixer_in_dec(xt, cbuft, qbuft, p, *, nb):
    ts, B, _ = xt.shape
    blk = lambda r, n: pl.BlockSpec((r, nb, n), lambda i: (0, i, 0))
    in_specs = [
        blk(ts, D_MODEL), blk(K_CONV - 1, D_MODEL), blk(K_SHORT - 1, D_QKV),
        _resident((1, D_MODEL)),
        _resident(p["wmain"].shape),
        _resident(p["wba"].shape),
        _resident(p["wdw"].shape),
        _resident((1, D_MODEL)),
        _resident((1, D_MODEL)),
        _resident((1, D_MODEL)),
        _resident(p["wcout"].shape),
        _resident(p["wsh"].shape),
        _resident((1, LANES)),
        _resident((1, LANES)),
    ]
    out_shape = [
        jax.ShapeDtypeStruct((ts, B, D_MODEL), BF16),
        jax.ShapeDtypeStruct((ts, B, D_QKV), BF16),
        jax.ShapeDtypeStruct((ts, B, D_MODEL), BF16),
        jax.ShapeDtypeStruct((ts, B, D_MODEL), BF16),
        jax.ShapeDtypeStruct((ts, B, LANES), F32),
        jax.ShapeDtypeStruct((ts, B, D_MODEL), F32),
        jax.ShapeDtypeStruct((ts, B, D_QKV), F32),
    ]
    out_specs = [blk(ts, D_MODEL), blk(ts, D_QKV), blk(ts, D_MODEL), blk(ts, D_MODEL),
                 blk(ts, LANES), blk(ts, D_MODEL), blk(ts, D_QKV)]
    return pl.pallas_call(
        functools.partial(_mixer_in_dec_kernel, ts=ts, nb=nb),
        grid=(B // nb,),
        in_specs=in_specs,
        out_specs=out_specs,
        out_shape=out_shape,
        scratch_shapes=[pltpu.VMEM((ts * nb, D_MODEL), BF16)],
        compiler_params=pltpu.CompilerParams(
            dimension_semantics=("arbitrary",), vmem_limit_bytes=VMEM_LIMIT),
        name="mixer_in_dec",
    )(xt, cbuft, qbuft, p["gmix"], p["wmain"], p["wba"], p["wdw"], p["bdw"], p["lng"], p["lnb"],
      p["wcout"], p["wsh"], p["alog_row"], p["dt_row"])


def _qkv_head(hd, part):
    return slice(part * D_MODEL + hd * D_HEAD, part * D_MODEL + (hd + 1) * D_HEAD)


def _intra_chunks(qkv_ref, bg_ref, bgt_ref, *, L, cps, ns):
    row = lax.broadcasted_iota(jnp.int32, (L, L), 0)
    col = lax.broadcasted_iota(jnp.int32, (L, L), 1)
    causal = row >= col
    strict = row > col
    tril = jnp.where(causal, 1.0, 0.0).astype(BF16)
    triu = jnp.where(row <= col, 1.0, 0.0).astype(BF16)
    eye = jnp.where(row == col, 1.0, 0.0).astype(F32)

    blocks = [(si, ci) for si in range(ns) for ci in range(cps)]
    items = [(si, ci, hd) for (si, ci) in blocks for hd in range(N_HEADS)]
    each = lambda f: {it: f(*it) for it in items}
    rows = lambda ci: slice(ci * L, (ci + 1) * L)

    bgc, gcum_c, gcum_r = {}, {}, {}
    for si, ci in blocks:
        bgc[si, ci] = bg_ref[si, rows(ci), :]
        c1, c2, c3 = _split3(bgc[si, ci])
        gcum_c[si, ci] = _dot(tril, c1) + _dot(tril, c2) + _dot(tril, c3)
        r1, r2, r3 = _split3(bgt_ref[si, :, rows(ci)])
        gcum_r[si, ci] = _dot(r1, triu) + _dot(r2, triu) + _dot(r3, triu)

    q = each(lambda si, ci, hd: qkv_ref[si, rows(ci), _qkv_head(hd, 0)])
    k = each(lambda si, ci, hd: qkv_ref[si, rows(ci), _qkv_head(hd, 1)])
    v = each(lambda si, ci, hd: qkv_ref[si, rows(ci), _qkv_head(hd, 2)])
    beta = each(lambda si, ci, hd: bgc[si, ci][:, hd:hd + 1])
    gcol = each(lambda si, ci, hd: gcum_c[si, ci][:, N_HEADS + hd:N_HEADS + hd + 1])
    grow = each(lambda si, ci, hd: gcum_r[si, ci][N_HEADS + hd:N_HEADS + hd + 1, :])
    glast = each(lambda si, ci, hd: gcum_c[si, ci][L - 1:L, N_HEADS + hd:N_HEADS + hd + 1])

    decay = each(lambda *it: jnp.where(
        causal, jnp.exp(jnp.where(causal, gcol[it] - grow[it], 0.0)), 0.0))
    kk = each(lambda *it: _dot_nt(k[it], k[it]))
    lm = each(lambda *it: jnp.where(strict, beta[it] * kk[it] * decay[it], 0.0))

    x = each(lambda *it: eye - lm[it])
    pw = lm
    n = 2
    while n < L:
        pb = each(lambda *it: pw[it].astype(BF16))
        pw = each(lambda *it: _dot(pb[it], pb[it]))
        x = each(lambda *it: x[it] + _dot(x[it].astype(BF16), pw[it].astype(BF16)))
        n *= 2

    eg = each(lambda *it: jnp.exp(gcol[it]))
    rhs = each(lambda *it: jnp.concatenate(
        [v[it].astype(F32) * beta[it], k[it].astype(F32) * (beta[it] * eg[it])], axis=1).astype(BF16))
    sol = each(lambda *it: _dot(x[it].astype(BF16), rhs[it]))
    attn = each(lambda *it: (_dot_nt(q[it], k[it]) * decay[it]).astype(BF16))
    q_dec = each(lambda *it: (q[it].astype(F32) * eg[it]).astype(BF16))
    k_dec = each(lambda *it: (k[it].astype(F32) * jnp.exp(glast[it] - gcol[it])).astype(BF16))
    last = each(lambda *it: jnp.exp(glast[it]))
    return sol, attn, (lambda it, u: u), q_dec, k_dec, last


def _intra_chunk_pairs(qkv_ref, bg_ref, bgt_ref, *, L, cps, ns):
    P = 2 * L
    rowp = lax.broadcasted_iota(jnp.int32, (L, P), 0)
    lanep = lax.broadcasted_iota(jnp.int32, (L, P), 1)
    left = lanep < L
    colp = jnp.where(left, lanep, lanep - L)
    causal = rowp >= colp
    strict = rowp > colp
    eye = jnp.where(rowp == colp, 1.0, 0.0).astype(F32)
    r2 = lax.broadcasted_iota(jnp.int32, (P, P), 0)
    c2 = lax.broadcasted_iota(jnp.int32, (P, P), 1)
    same = (r2 < L) == (c2 < L)
    tril = jnp.where(jnp.logical_and(same, r2 >= c2), 1.0, 0.0).astype(BF16)
    triu = jnp.where(jnp.logical_and(same, r2 <= c2), 1.0, 0.0).astype(BF16)

    pairs = [(si, m) for si in range(ns) for m in range(cps // 2)]
    items = [(si, m, hd) for (si, m) in pairs for hd in range(N_HEADS)]
    each = lambda f: {it: f(*it) for it in items}
    rows = lambda m: slice(m * P, (m + 1) * P)
    side_by_side = lambda a: jnp.where(left, a[0:L], a[L:P])
    block_diag = lambda a: jnp.concatenate(
        [jnp.where(left, a, 0.0), jnp.where(left, 0.0, a)], axis=0).astype(BF16)

    bgc, gcum_c, gcum_r = {}, {}, {}
    for si, m in pairs:
        bgc[si, m] = bg_ref[si, rows(m), :]
        c1, c2_, c3 = _split3(bgc[si, m])
        gcum_c[si, m] = _dot(tril, c1) + _dot(tril, c2_) + _dot(tril, c3)
        r1, r2_, r3 = _split3(bgt_ref[si, :, rows(m)])
        gcum_r[si, m] = _dot(r1, triu) + _dot(r2_, triu) + _dot(r3, triu)

    q = each(lambda si, m, hd: qkv_ref[si, rows(m), _qkv_head(hd, 0)])
    k = each(lambda si, m, hd: qkv_ref[si, rows(m), _qkv_head(hd, 1)])
    v = each(lambda si, m, hd: qkv_ref[si, rows(m), _qkv_head(hd, 2)])
    beta = each(lambda si, m, hd: bgc[si, m][:, hd:hd + 1])
    gcol = each(lambda si, m, hd: gcum_c[si, m][:, N_HEADS + hd:N_HEADS + hd + 1])
    grow = each(lambda si, m, hd: gcum_r[si, m][N_HEADS + hd:N_HEADS + hd + 1, :])

    decay = each(lambda *it: jnp.where(
        causal, jnp.exp(jnp.where(causal, side_by_side(gcol[it]) - grow[it], 0.0)), 0.0))
    kk = each(lambda *it: side_by_side(_dot_nt(k[it], k[it])))
    lm = each(lambda *it: jnp.where(strict, side_by_side(beta[it]) * kk[it] * decay[it], 0.0))

    x = each(lambda *it: eye - lm[it])
    pw = each(lambda *it: _dot(lm[it].astype(BF16), block_diag(lm[it])))
    n = 2
    while n < L:
        w = each(lambda *it: block_diag(pw[it]))
        if 2 * n < L:
            r = each(lambda *it: _dot(jnp.concatenate([x[it], pw[it]], axis=0).astype(BF16), w[it]))
            x = each(lambda *it: x[it] + r[it][0:L])
            pw = each(lambda *it: r[it][L:P])
        else:
            x = each(lambda *it: x[it] + _dot(x[it].astype(BF16), w[it]))
        n *= 2

    eg = each(lambda *it: jnp.exp(gcol[it]))
    rhs = each(lambda *it: jnp.concatenate(
        [v[it].astype(F32) * beta[it], k[it].astype(F32) * (beta[it] * eg[it])], axis=1).astype(BF16))
    sol2 = each(lambda *it: _dot(block_diag(x[it]), rhs[it]))
    attn2 = each(lambda *it: side_by_side(_dot_nt(q[it], k[it])) * decay[it])
    q_dec2 = each(lambda *it: (q[it].astype(F32) * eg[it]).astype(BF16))

    sol, attn, q_dec, k_dec, last = {}, {}, {}, {}, {}
    for it in items:
        si, m, hd = it
        for half in range(2):
            key = (si, 2 * m + half, hd)
            hr = slice(half * L, (half + 1) * L)
            glast = gcum_c[si, m][(half + 1) * L - 1:(half + 1) * L, N_HEADS + hd:N_HEADS + hd + 1]
            sol[key] = sol2[it][hr]
            attn[key] = jnp.where(left if half == 0 else jnp.logical_not(left), attn2[it], 0.0).astype(BF16)
            q_dec[key] = q_dec2[it][hr]
            k_dec[key] = (k[it][hr].astype(F32) * jnp.exp(glast - gcol[it][hr])).astype(BF16)
            last[key] = jnp.exp(glast)

    def attn_rhs(key, u):
        z = jnp.zeros_like(u)
        return jnp.concatenate([u, z] if key[1] % 2 == 0 else [z, u], axis=0)

    return sol, attn, attn_rhs, q_dec, k_dec, last


def _delta_kernel(qkv_ref, gz_ref, bg_ref, bgt_ref, s0_ref, ghead_ref, og_ref, sout_ref, s_ref,
                  *, L, cps, ns):
    c = pl.program_id(1)
    nc = pl.num_programs(1)

    @pl.when(c == 0)
    def _():
        for si in range(ns):
            s_ref[si] = s0_ref[min(si, s0_ref.shape[0] - 1)]

    ghead = ghead_ref[...]
    rows = lambda ci: slice(ci * L, (ci + 1) * L)
    head = lambda hd: slice(hd * D_HEAD, (hd + 1) * D_HEAD)
    intra = _intra_chunk_pairs if (cps % 2 == 0 and 2 * L == LANES) else _intra_chunks
    sol, attn, attn_rhs, q_dec, k_dec, last = intra(qkv_ref, bg_ref, bgt_ref, L=L, cps=cps, ns=ns)

    state = {(si, hd): s_ref[si, hd] for si in range(ns) for hd in range(N_HEADS)}
    for ci in range(cps):
        cur = [(si, ci, hd) for si in range(ns) for hd in range(N_HEADS)]
        now = lambda f: {it: f(*it) for it in cur}
        sb = now(lambda si, ci, hd: state[si, hd].astype(BF16))
        ks = now(lambda *it: _dot(jnp.concatenate(
            [sol[it][:, D_HEAD:].astype(BF16), q_dec[it]], axis=0), sb[it]))
        ub = now(lambda *it: (sol[it][:, :D_HEAD] - ks[it][0:L]).astype(BF16))
        o = now(lambda *it: ks[it][L:2 * L] + _dot(attn[it], attn_rhs(it, ub[it])))
        for it in cur:
            si, _, hd = it
            state[si, hd] = state[si, hd] * last[it] + _dot_tn(k_dec[it], ub[it])
            on = _rms(o[it], ghead) * gz_ref[si, rows(ci), head(hd)].astype(F32)
            og_ref[si, rows(ci), head(hd)] = on.astype(BF16)
    for (si, hd), s in state.items():
        s_ref[si, hd] = s

    @pl.when(c == nc - 1)
    def _():
        sout_ref[...] = s_ref[...]


def _delta(qkv, gz, bg, bgt, s0, ghead, *, L, cps, ns):
    B, T, _ = qkv.shape
    tb = L * cps
    bs = s0.shape[0]
    assert bs == B or (bs == 1 and ns == 1)
    s_spec = (pl.BlockSpec((ns, N_HEADS, D_HEAD, D_HEAD), lambda b, c: (b, 0, 0, 0)) if bs == B else
              pl.BlockSpec((1, N_HEADS, D_HEAD, D_HEAD), lambda b, c: (0, 0, 0, 0)))
    tile = lambda n: pl.BlockSpec((ns, tb, n), lambda b, c: (b, c, 0))
    return pl.pallas_call(
        functools.partial(_delta_kernel, L=L, cps=cps, ns=ns),
        grid=(B // ns, T // tb),
        in_specs=[
            tile(D_QKV), tile(D_MODEL), tile(LANES),
            pl.BlockSpec((ns, 2 * N_HEADS, tb), lambda b, c: (b, 0, c)),
            s_spec,
            _resident((1, D_HEAD)),
        ],
        out_specs=[
            tile(D_MODEL),
            pl.BlockSpec((ns, N_HEADS, D_HEAD, D_HEAD), lambda b, c: (b, 0, 0, 0)),
        ],
        out_shape=[
            jax.ShapeDtypeStruct((B, T, D_MODEL), BF16),
            jax.ShapeDtypeStruct((B, N_HEADS, D_HEAD, D_HEAD), F32),
        ],
        scratch_shapes=[pltpu.VMEM((ns, N_HEADS, D_HEAD, D_HEAD), F32)],
        compiler_params=pltpu.CompilerParams(
            dimension_semantics=("arbitrary", "arbitrary"), vmem_limit_bytes=VMEM_LIMIT),
        name="delta_rule",
    )(qkv, gz, bg, bgt, s0, ghead)


def _mixer_out_kernel(x_ref, a_ref, og_ref, sgd_ref, wdout_ref, wo_ref, gmlp_ref, wup_ref, wdown_ref,
                      gfin_ref, y_ref):
    dn = _dot(og_ref[...], wdout_ref[...])
    mix = a_ref[...].astype(F32) + sgd_ref[...].astype(F32) * dn
    x2 = x_ref[...] + _dot(mix.astype(BF16), wo_ref[...])
    hn = _rms(x2, gmlp_ref[...]).astype(BF16)
    acc = x2
    for j in range(0, D_FF, D_MODEL):
        hm = jnp.maximum(_dot(hn, wup_ref[:, j:j + D_MODEL]), 0.0)
        acc = acc + _dot((hm * hm).astype(BF16), wdown_ref[j:j + D_MODEL, :])
    y_ref[...] = _rms(acc, gfin_ref[...])


def _mixer_out(x, a, og, sgd, p, *, tm):
    n = x.shape[0]
    tile = pl.BlockSpec((tm, D_MODEL), lambda i: (i, 0))
    return pl.pallas_call(
        _mixer_out_kernel,
        grid=(n // tm,),
        in_specs=[tile, tile, tile, tile,
                  _resident(p["wdout"].shape), _resident(p["wo"].shape), _resident((1, D_MODEL)),
                  _resident(p["wup"].shape), _resident(p["wdown"].shape), _resident((1, D_MODEL))],
        out_specs=tile,
        out_shape=jax.ShapeDtypeStruct((n, D_MODEL), F32),
        compiler_params=pltpu.CompilerParams(
            dimension_semantics=("arbitrary",), vmem_limit_bytes=VMEM_LIMIT),
        name="mixer_out",
    )(x, a, og, sgd, p["wdout"], p["wo"], p["gmlp"], p["wup"], p["wdown"], p["gfin"])


def _prep_params(g_mix, w_in, w_dw, b_dw, ln_g, ln_b, w_cout, w_short, a_log, dt_bias, g_head,
                 w_dout, w_o, g_mlp, w_up, w_down, g_final):
    w = w_in[0]
    n_main = 2 * D_MODEL + D_QKV + D_MODEL
    ba = w[:, n_main:n_main + 2 * N_HEADS]
    pad16 = lambda v: jnp.pad(v.astype(F32), (N_HEADS, 0))
    rows8 = lambda v: jnp.broadcast_to(v[:, None, :], (v.shape[0], SUBLANES, v.shape[1]))
    return dict(
        wmain=jnp.concatenate([w[:, :n_main], w[:, n_main + 2 * N_HEADS:]], axis=1).astype(BF16),
        wba=jnp.pad(ba, ((0, 0), (0, LANES - 2 * N_HEADS))).astype(BF16),
        wbat=ba.T.astype(BF16),
        gmix=g_mix[0].reshape(1, D_MODEL),
        wdw=w_dw[0], wdw8=rows8(w_dw[0]), bdw=b_dw[0].reshape(1, D_MODEL),
        lng=ln_g[0].reshape(1, D_MODEL), lnb=ln_b[0].reshape(1, D_MODEL),
        wcout=w_cout[0].astype(BF16), wsh=w_short[0], wsh8=rows8(w_short[0]),
        alog_row=jnp.pad(pad16(a_log[0]), (0, LANES - 2 * N_HEADS)).reshape(1, LANES),
        dt_row=jnp.pad(pad16(dt_bias[0]), (0, LANES - 2 * N_HEADS)).reshape(1, LANES),
        alog_col=pad16(a_log[0]).reshape(2 * N_HEADS, 1),
        dt_col=pad16(dt_bias[0]).reshape(2 * N_HEADS, 1),
        ghead=g_head[0].reshape(1, D_HEAD),
        wdout=w_dout[0].astype(BF16), wo=w_o[0].astype(BF16), gmlp=g_mlp[0].reshape(1, D_MODEL),
        wup=w_up[0].astype(BF16), wdown=w_down[0].astype(BF16), gfin=g_final.reshape(1, D_MODEL),
    )


def kernel(x_prompt, x_sample, state_conv, state_qkv_conv, state_delta, meta_tokens, g_mix, w_in, w_dw, b_dw, ln_g, ln_b, w_cout, w_short, a_log, dt_bias, g_head, w_dout, w_o, g_mlp, w_up, w_down, g_final):
    p = _prep_params(g_mix, w_in, w_dw, b_dw, ln_g, ln_b, w_cout, w_short, a_log, dt_bias, g_head,
                     w_dout, w_o, g_mlp, w_up, w_down, g_final)
    bp, seq, _ = x_prompt.shape
    bd, ts, _ = x_sample.shape

    zc = jnp.zeros((1, CONV_HALO, D_MODEL), F32)
    zq = jnp.zeros((1, SHORT_HALO, D_QKV), F32)
    zs = jnp.zeros((1, N_HEADS, D_HEAD, D_HEAD), F32)
    xm = meta_tokens.astype(F32)[None]
    _, qkv_m, gz_m, _, bg_m, bgt_m, cv_m, qk_m = _mixer_in_seq(xm, zc, zq, p, tm=N_META, row_block=N_META)
    _, s_m = _delta(qkv_m, gz_m, bg_m, bgt_m, zs, p["ghead"], L=N_META, cps=1, ns=1)

    a_p, qkv_p, gz_p, sgd_p, bg_p, bgt_p, cv_p, qk_p = _mixer_in_seq(
        x_prompt, cv_m, qk_m, p, tm=TM_IN_PROMPT, row_block=CONV_ROW_BLOCK)
    og_p, s_p = _delta(qkv_p, gz_p, bg_p, bgt_p, s_m, p["ghead"], L=DELTA_CHUNK,
                       cps=DELTA_CHUNKS_PER_STEP, ns=1)
    n_p = bp * seq
    flat = lambda v: v.reshape(n_p, D_MODEL)
    y_prompt = _mixer_out(flat(x_prompt), flat(a_p), flat(og_p), flat(sgd_p), p,
                          tm=TM_OUT_PROMPT).reshape(bp, seq, D_MODEL)

    tmaj = lambda v: jnp.transpose(v, (1, 0, 2))
    a_s, qkv_s, gz_s, sgd_s, bg_s, glu_s, raw_s = _mixer_in_dec(
        tmaj(x_sample), tmaj(state_conv[0]), tmaj(state_qkv_conv[0]), p, nb=DEC_BATCH_BLOCK)
    padt = lambda v: jnp.pad(tmaj(v), ((0, 0), (0, DEC_PAD - ts), (0, 0)))
    bgt_s = jnp.pad(jnp.transpose(bg_s[:, :, :2 * N_HEADS], (1, 2, 0)), ((0, 0), (0, 0), (0, DEC_PAD - ts)))
    og_s, s_s = _delta(padt(qkv_s), padt(gz_s), padt(bg_s), bgt_s, state_delta[0], p["ghead"],
                       L=DEC_PAD, cps=1, ns=DEC_SEQS_PER_STEP)
    n_s = bd * ts
    y_sample = _mixer_out(x_sample.reshape(n_s, D_MODEL), tmaj(a_s).reshape(n_s, D_MODEL),
                          og_s[:, :ts].reshape(n_s, D_MODEL), tmaj(sgd_s).reshape(n_s, D_MODEL), p,
                          tm=TM_OUT_DECODE).reshape(bd, ts, D_MODEL)

    nbuf, nsh = K_CONV - 1, K_SHORT - 1
    prompt_conv = cv_p[:, CONV_HALO - nbuf:][None]
    prompt_qkv_conv = qk_p[:, SHORT_HALO - nsh:][None]
    sample_conv = jnp.concatenate([state_conv[0][:, ts:], tmaj(glu_s)], axis=1)[None]
    sample_qkv_conv = tmaj(raw_s[ts - nsh:])[None]
    return (y_prompt, y_sample, prompt_conv, prompt_qkv_conv, s_p[None],
            sample_conv, sample_qkv_conv, s_s[None])
```

```python
import functools

import jax
import jax.numpy as jnp
from jax import lax
from jax.experimental import pallas as pl
from jax.experimental.pallas import tpu as pltpu

F32 = jnp.float32
BF16 = jnp.bfloat16

D_MODEL = 1024
N_HEADS = 8
D_HEAD = 128
D_QKV = 3 * D_MODEL
D_FF = 4 * D_MODEL
K_CONV = 31
K_SHORT = 4
N_META = 16
EPS = 1e-6

CONV_HALO = 32
SHORT_HALO = 8
LANES = 128
SUBLANES = 8
COL_BLOCK = 512
LN_ROWS = 16
ROW_STRIDE = 2
N_SLABS = D_MODEL // LANES
N_QKV_SLABS = D_QKV // LANES

OFF_GLU_A, OFF_GLU_B, OFF_QKV, OFF_Z, OFF_GC, OFF_GD = 0, 1024, 2048, 5120, 6144, 7168

VMEM_LIMIT = 56 * 1024 * 1024

TM_IN_PROMPT = 256
TM_OUT_PROMPT = 512
TM_OUT_DECODE = 256
CONV_ROW_BLOCK = 64
DELTA_CHUNK = 64
DELTA_CHUNKS_PER_STEP = 8
DEC_PAD = 16
DEC_SEQS_PER_STEP = 4
DEC_BATCH_BLOCK = 32


def _dot(a, b):
    return jnp.dot(a, b, preferred_element_type=F32)


def _dot_nt(a, b):
    return lax.dot_general(a, b, (((1,), (1,)), ((), ())), preferred_element_type=F32)


def _dot_tn(a, b):
    return lax.dot_general(a, b, (((0,), (0,)), ((), ())), preferred_element_type=F32)


def _sigmoid(x):
    return 0.5 * jnp.tanh(0.5 * x) + 0.5


def _silu(x):
    hx = 0.5 * x
    return hx * jnp.tanh(hx) + hx


def _softplus(x):
    return jnp.maximum(x, 0.0) + jnp.log(1.0 + jnp.exp(-jnp.abs(x)))


def _rms(x, g):
    return x * lax.rsqrt(jnp.mean(x * x, axis=-1, keepdims=True) + EPS) * g


def _layer_norm_silu(c, g, b):
    mu = jnp.mean(c, axis=-1, keepdims=True)
    d = c - mu
    var = jnp.mean(d * d, axis=-1, keepdims=True)
    return _silu(d * lax.rsqrt(var + EPS) * g + b)


def _split3(x):
    x1 = x.astype(BF16)
    r1 = x - x1.astype(F32)
    x2 = r1.astype(BF16)
    x3 = (r1 - x2.astype(F32)).astype(BF16)
    return x1, x2, x3


def _resident(shape):
    nd = len(shape)
    return pl.BlockSpec(shape, lambda *_: (0,) * nd, pipeline_mode=pl.Buffered(1))


def _beta_g_rows(raw, alog_row, dt_row):
    lane = lax.broadcasted_iota(jnp.int32, raw.shape, 1)
    g = -jnp.exp(alog_row) * _softplus(raw + dt_row)
    return jnp.where(lane < N_HEADS, _sigmoid(raw), g)


def _qkv_post(acc, cb):
    y = _silu(acc)
    if cb < 2 * N_HEADS:
        y = y * lax.rsqrt(jnp.sum(y * y, axis=-1, keepdims=True) + 1e-6)
        if cb < N_HEADS:
            y = y * (D_HEAD ** -0.5)
    return y


def _interleave(main, side):
    done = 0
    for idx, task in enumerate(main):
        task()
        while done * len(main) < (idx + 1) * len(side):
            side[done]()
            done += 1
    for task in side[done:]:
        task()


def _row_groups(lo, hi):
    span = ROW_STRIDE * SUBLANES
    return [lo + m + ph for m in range(0, hi - lo, span) for ph in range(ROW_STRIDE)]


def _strided_conv(src_ref, dst_ref, w_ref, slab, first, n_taps, rows):
    ls = slice(slab * LANES, (slab + 1) * LANES)
    starts = _row_groups(*rows)
    accs = [None] * len(starts)
    for i in range(n_taps):
        w = w_ref[i, :, ls]
        for gi, s in enumerate(starts):
            term = w * src_ref[slab, pl.ds(s + first + i, SUBLANES, stride=ROW_STRIDE), :]
            accs[gi] = term if accs[gi] is None else accs[gi] + term
    for gi, s in enumerate(starts):
        dst_ref[slab, pl.ds(s, SUBLANES, stride=ROW_STRIDE), :] = accs[gi]


def _mixer_in_seq_kernel(x_ref, cbuf_ref, qbuf_ref, gmix_ref, w_ref, wba_ref, wbat_ref, wdw_ref,
                         bdw_ref, lng_ref, lnb_ref, wcout_ref, wsh_ref, alr_ref, dtr_ref, alc_ref,
                         dtc_ref, a_ref, qkv_ref, gz_ref, sgd_ref, bg_ref, bgt_ref, ncv_ref, nqk_ref,
                         ext_ref, yc_ref, qext_ref, qy_ref, c_ref, *, tm, row_block):
    t = pl.program_id(1)
    nt = pl.num_programs(1)
    lanes = lambda k: slice(k * LANES, (k + 1) * LANES)

    @pl.when(t == 0)
    def _():
        for lb in range(N_SLABS):
            ext_ref[lb, 0:CONV_HALO, :] = cbuf_ref[0, :, lanes(lb)]
        for cb in range(N_QKV_SLABS):
            qext_ref[cb, 0:SHORT_HALO, :] = qbuf_ref[0, :, lanes(cb)]

    @pl.when(t > 0)
    def _():
        for lb in range(N_SLABS):
            ext_ref[lb, 0:CONV_HALO, :] = ext_ref[lb, tm:tm + CONV_HALO, :]
        for cb in range(N_QKV_SLABS):
            qext_ref[cb, 0:SHORT_HALO, :] = qext_ref[cb, tm:tm + SHORT_HALO, :]

    x = x_ref[0]
    h = _rms(x, gmix_ref[...]).astype(BF16)

    for j in range(0, D_MODEL, COL_BLOCK):
        a = _dot(h, w_ref[:, OFF_GLU_A + j:OFF_GLU_A + j + COL_BLOCK])
        b = _dot(h, w_ref[:, OFF_GLU_B + j:OFF_GLU_B + j + COL_BLOCK])
        glu = a * _sigmoid(b)
        for c in range(0, COL_BLOCK, LANES):
            ext_ref[(j + c) // LANES, CONV_HALO:CONV_HALO + tm, :] = glu[:, c:c + LANES]

    def qkv_cols(j):
        raw = _dot(h, w_ref[:, OFF_QKV + j:OFF_QKV + j + COL_BLOCK])
        for c in range(0, COL_BLOCK, LANES):
            qext_ref[(j + c) // LANES, SHORT_HALO:SHORT_HALO + tm, :] = raw[:, c:c + LANES]

    def short_conv(cb):
        _strided_conv(qext_ref, qy_ref, wsh_ref, cb, SHORT_HALO - (K_SHORT - 1), K_SHORT, (0, tm))
        qkv_ref[0, :, lanes(cb)] = _qkv_post(qy_ref[cb], cb).astype(BF16)

    def gated(out_ref, off, act, j):
        cs = slice(j, j + COL_BLOCK)
        out_ref[0, :, cs] = act(_dot(h, w_ref[:, off + j:off + j + COL_BLOCK])).astype(BF16)

    def beta_g():
        bg_ref[0] = _beta_g_rows(_dot(h, wba_ref[...]), alr_ref[...], dtr_ref[...])
        rawt = _dot_nt(wbat_ref[...], h)
        srow = lax.broadcasted_iota(jnp.int32, rawt.shape, 0)
        gt = -jnp.exp(alc_ref[...]) * _softplus(rawt + dtc_ref[...])
        bgt_ref[0] = jnp.where(srow < N_HEADS, _sigmoid(rawt), gt)

    side = []
    for j in range(0, D_QKV, COL_BLOCK):
        side.append(functools.partial(qkv_cols, j))
        side += [functools.partial(short_conv, cb)
                 for cb in range(j // LANES, (j + COL_BLOCK) // LANES)]
    for j in range(0, D_MODEL, COL_BLOCK):
        side.append(functools.partial(gated, gz_ref, OFF_Z, lambda v: v, j))
        side.append(functools.partial(gated, sgd_ref, OFF_GD, lambda v: v, j))
    side.append(beta_g)

    conv = [functools.partial(_strided_conv, ext_ref, yc_ref, wdw_ref, lb, CONV_HALO - (K_CONV - 1),
                              K_CONV, (rb, rb + row_block))
            for rb in range(0, tm, row_block) for lb in range(N_SLABS)]
    _interleave(conv, side)
    for rb in range(0, tm, LN_ROWS):
        y = jnp.concatenate([yc_ref[lb, rb:rb + LN_ROWS, :] for lb in range(N_SLABS)], axis=1)
        y = y + bdw_ref[...]
        c_ref[rb:rb + LN_ROWS, :] = _layer_norm_silu(y, lng_ref[...], lnb_ref[...]).astype(BF16)

    for j in range(0, D_MODEL, COL_BLOCK):
        cs = slice(j, j + COL_BLOCK)
        conv_out = _dot(c_ref[...], wcout_ref[:, cs])
        gate_c = _dot(h, w_ref[:, OFF_GC + j:OFF_GC + j + COL_BLOCK])
        a_ref[0, :, cs] = (_sigmoid(gate_c) * conv_out).astype(BF16)

    @pl.when(t == nt - 1)
    def _():
        for lb in range(N_SLABS):
            ncv_ref[0, :, lanes(lb)] = ext_ref[lb, tm:tm + CONV_HALO, :]
        for cb in range(N_QKV_SLABS):
            nqk_ref[0, :, lanes(cb)] = qext_ref[cb, tm:tm + SHORT_HALO, :]


def _mixer_in_seq(x, cbuf, qbuf, p, *, tm, row_block):
    B, T, _ = x.shape
    nt = T // tm
    bs = cbuf.shape[0]
    state_idx = (lambda b, t: (b, 0, 0)) if bs == B else (lambda b, t: (0, 0, 0))
    tile = lambda n: pl.BlockSpec((1, tm, n), lambda b, t: (b, t, 0))
    in_specs = [
        tile(D_MODEL),
        pl.BlockSpec((1, CONV_HALO, D_MODEL), state_idx),
        pl.BlockSpec((1, SHORT_HALO, D_QKV), state_idx),
        _resident((1, D_MODEL)),
        _resident(p["wmain"].shape),
        _resident(p["wba"].shape),
        _resident(p["wbat"].shape),
        _resident(p["wdw8"].shape),
        _resident((1, D_MODEL)),
        _resident((1, D_MODEL)),
        _resident((1, D_MODEL)),
        _resident(p["wcout"].shape),
        _resident(p["wsh8"].shape),
        _resident((1, LANES)),
        _resident((1, LANES)),
        _resident((2 * N_HEADS, 1)),
        _resident((2 * N_HEADS, 1)),
    ]
    out_shape = [
        jax.ShapeDtypeStruct((B, T, D_MODEL), BF16),
        jax.ShapeDtypeStruct((B, T, D_QKV), BF16),
        jax.ShapeDtypeStruct((B, T, D_MODEL), BF16),
        jax.ShapeDtypeStruct((B, T, D_MODEL), BF16),
        jax.ShapeDtypeStruct((B, T, LANES), F32),
        jax.ShapeDtypeStruct((B, 2 * N_HEADS, T), F32),
        jax.ShapeDtypeStruct((B, CONV_HALO, D_MODEL), F32),
        jax.ShapeDtypeStruct((B, SHORT_HALO, D_QKV), F32),
    ]
    out_specs = [
        tile(D_MODEL), tile(D_QKV), tile(D_MODEL), tile(D_MODEL), tile(LANES),
        pl.BlockSpec((1, 2 * N_HEADS, tm), lambda b, t: (b, 0, t)),
        pl.BlockSpec((1, CONV_HALO, D_MODEL), lambda b, t: (b, 0, 0)),
        pl.BlockSpec((1, SHORT_HALO, D_QKV), lambda b, t: (b, 0, 0)),
    ]
    return pl.pallas_call(
        functools.partial(_mixer_in_seq_kernel, tm=tm, row_block=row_block),
        grid=(B, nt),
        in_specs=in_specs,
        out_specs=out_specs,
        out_shape=out_shape,
        scratch_shapes=[
            pltpu.VMEM((N_SLABS, tm + CONV_HALO, LANES), F32),
            pltpu.VMEM((N_SLABS, tm, LANES), F32),
            pltpu.VMEM((N_QKV_SLABS, tm + SHORT_HALO, LANES), F32),
            pltpu.VMEM((N_QKV_SLABS, tm, LANES), F32),
            pltpu.VMEM((tm, D_MODEL), BF16),
        ],
        compiler_params=pltpu.CompilerParams(
            dimension_semantics=("arbitrary", "arbitrary"), vmem_limit_bytes=VMEM_LIMIT),
        name="mixer_in_seq",
    )(x, cbuf, qbuf, p["gmix"], p["wmain"], p["wba"], p["wbat"], p["wdw8"], p["bdw"], p["lng"],
      p["lnb"], p["wcout"], p["wsh8"], p["alog_row"], p["dt_row"], p["alog_col"], p["dt_col"])


def _mixer_in_dec_kernel(x_ref, cbuf_ref, qbuf_ref, gmix_ref, w_ref, wba_ref, wdw_ref, bdw_ref,
                         lng_ref, lnb_ref, wcout_ref, wsh_ref, alr_ref, dtr_ref,
                         a_ref, qkv_ref, z_ref, gd_ref, bg_ref, ncv_ref, nqk_ref,
                         glu_ref, raw_ref, c_ref, *, ts, nb):
    nbuf, nsh = K_CONV - 1, K_SHORT - 1
    pos = lambda t: slice(t * nb, (t + 1) * nb)
    col = lambda t, n: slice(t * n, (t + 1) * n)

    x = jnp.concatenate([x_ref[:, col(t, D_MODEL)] for t in range(ts)], axis=0)
    h = _rms(x, gmix_ref[...]).astype(BF16)

    for j in range(0, D_MODEL, COL_BLOCK):
        a = _dot(h, w_ref[:, OFF_GLU_A + j:OFF_GLU_A + j + COL_BLOCK])
        b = _dot(h, w_ref[:, OFF_GLU_B + j:OFF_GLU_B + j + COL_BLOCK])
        glu_ref[:, j:j + COL_BLOCK] = a * _sigmoid(b)

    for t in range(ts):
        acc = jnp.zeros((nb, D_MODEL), F32) + bdw_ref[...]
        for i in range(K_CONV):
            j = t + i
            row = cbuf_ref[:, col(j, D_MODEL)] if j < nbuf else glu_ref[pos(j - nbuf), :]
            acc = acc + wdw_ref[i:i + 1, :] * row
        c_ref[pos(t), :] = _layer_norm_silu(acc, lng_ref[...], lnb_ref[...]).astype(BF16)

    conv_out = _dot(c_ref[...], wcout_ref[...])
    gate_c = _dot(h, w_ref[:, OFF_GC:OFF_GC + D_MODEL])
    a_val = (_sigmoid(gate_c) * conv_out).astype(BF16)

    for j in range(0, D_QKV, COL_BLOCK):
        raw_ref[:, j:j + COL_BLOCK] = _dot(h, w_ref[:, OFF_QKV + j:OFF_QKV + j + COL_BLOCK])
    for t in range(ts):
        for cb in range(D_QKV // LANES):
            sl = slice(cb * LANES, (cb + 1) * LANES)
            acc = None
            for i in range(K_SHORT):
                j = t + i
                row = (qbuf_ref[:, j * D_QKV + cb * LANES:j * D_QKV + (cb + 1) * LANES] if j < nsh
                       else raw_ref[pos(j - nsh), sl])
                term = wsh_ref[i:i + 1, sl] * row
                acc = term if acc is None else acc + term
            qkv_ref[:, t * D_QKV + cb * LANES:t * D_QKV + (cb + 1) * LANES] = _qkv_post(
                acc, cb).astype(BF16)

    z_val = _dot(h, w_ref[:, OFF_Z:OFF_Z + D_MODEL]).astype(BF16)
    gd_val = _dot(h, w_ref[:, OFF_GD:OFF_GD + D_MODEL]).astype(BF16)
    bg_val = _beta_g_rows(_dot(h, wba_ref[...]), alr_ref[...], dtr_ref[...])
    for t in range(ts):
        a_ref[:, col(t, D_MODEL)] = a_val[pos(t)]
        z_ref[:, col(t, D_MODEL)] = z_val[pos(t)]
        gd_ref[:, col(t, D_MODEL)] = gd_val[pos(t)]
        bg_ref[:, col(t, LANES)] = bg_val[pos(t)]

    ncv_ref[:, 0:(nbuf - ts) * D_MODEL] = cbuf_ref[:, ts * D_MODEL:nbuf * D_MODEL]
    for t in range(ts):
        ncv_ref[:, col(nbuf - ts + t, D_MODEL)] = glu_ref[pos(t), :]
    for u in range(nsh):
        nqk_ref[:, col(u, D_QKV)] = raw_ref[pos(ts - nsh + u), :]


def _mixer_in_dec(x2, cbuf2, qbuf2, p, *, ts, nb):
    B = x2.shape[0]
    assert K_SHORT - 1 <= ts <= K_CONV - 1
    blk = lambda n: pl.BlockSpec((nb, n), lambda i: (i, 0))
    in_specs = [
        blk(ts * D_MODEL), blk((K_CONV - 1) * D_MODEL), blk((K_SHORT - 1) * D_QKV),
        _resident((1, D_MODEL)),
        _resident(p["wmain"].shape),
        _resident(p["wba"].shape),
        _resident(p["wdw"].shape),
        _resident((1, D_MODEL)),
        _resident((1, D_MODEL)),
        _resident((1, D_MODEL)),
        _resident(p["wcout"].shape),
        _resident(p["wsh"].shape),
        _resident((1, LANES)),
        _resident((1, LANES)),
    ]
    widths = [(ts * D_MODEL, BF16),
              (ts * D_QKV, BF16),
              (ts * D_MODEL, BF16),
              (ts * D_MODEL, BF16),
              (ts * LANES, F32),
              ((K_CONV - 1) * D_MODEL, F32),
              ((K_SHORT - 1) * D_QKV, F32)]
    return pl.pallas_call(
        functools.partial(_mixer_in_dec_kernel, ts=ts, nb=nb),
        grid=(B // nb,),
        in_specs=in_specs,
        out_specs=[blk(n) for n, _ in widths],
        out_shape=[jax.ShapeDtypeStruct((B, n), dt) for n, dt in widths],
        scratch_shapes=[pltpu.VMEM((ts * nb, D_MODEL), F32), pltpu.VMEM((ts * nb, D_QKV), F32),
                        pltpu.VMEM((ts * nb, D_MODEL), BF16)],
        compiler_params=pltpu.CompilerParams(
            dimension_semantics=("arbitrary",), vmem_limit_bytes=VMEM_LIMIT),
        name="mixer_in_dec",
    )(x2, cbuf2, qbuf2, p["gmix"], p["wmain"], p["wba"], p["wdw"], p["bdw"], p["lng"], p["lnb"],
      p["wcout"], p["wsh"], p["alog_row"], p["dt_row"])


def _qkv_head(hd, part):
    return slice(part * D_MODEL + hd * D_HEAD, part * D_MODEL + (hd + 1) * D_HEAD)


def _intra_chunks(qkv_ref, bg_ref, bgt_ref, *, L, cps, ns):
    row = lax.broadcasted_iota(jnp.int32, (L, L), 0)
    col = lax.broadcasted_iota(jnp.int32, (L, L), 1)
    causal = row >= col
    strict = row > col
    tril = jnp.where(causal, 1.0, 0.0).astype(BF16)
    triu = jnp.where(row <= col, 1.0, 0.0).astype(BF16)
    eye = jnp.where(row == col, 1.0, 0.0).astype(F32)

    blocks = [(si, ci) for si in range(ns) for ci in range(cps)]
    items = [(si, ci, hd) for (si, ci) in blocks for hd in range(N_HEADS)]
    each = lambda f: {it: f(*it) for it in items}
    rows = lambda ci: slice(ci * L, (ci + 1) * L)

    bgc, gcum_c, gcum_r = {}, {}, {}
    for si, ci in blocks:
        bgc[si, ci] = bg_ref[si, rows(ci), :]
        c1, c2, c3 = _split3(bgc[si, ci])
        gcum_c[si, ci] = _dot(tril, c1) + _dot(tril, c2) + _dot(tril, c3)
        r1, r2, r3 = _split3(bgt_ref[si, :, rows(ci)])
        gcum_r[si, ci] = _dot(r1, triu) + _dot(r2, triu) + _dot(r3, triu)

    q = each(lambda si, ci, hd: qkv_ref[si, rows(ci), _qkv_head(hd, 0)])
    k = each(lambda si, ci, hd: qkv_ref[si, rows(ci), _qkv_head(hd, 1)])
    v = each(lambda si, ci, hd: qkv_ref[si, rows(ci), _qkv_head(hd, 2)])
    beta = each(lambda si, ci, hd: bgc[si, ci][:, hd:hd + 1])
    gcol = each(lambda si, ci, hd: gcum_c[si, ci][:, N_HEADS + hd:N_HEADS + hd + 1])
    grow = each(lambda si, ci, hd: gcum_r[si, ci][N_HEADS + hd:N_HEADS + hd + 1, :])
    glast = each(lambda si, ci, hd: gcum_c[si, ci][L - 1:L, N_HEADS + hd:N_HEADS + hd + 1])

    decay = each(lambda *it: jnp.where(
        causal, jnp.exp(jnp.where(causal, gcol[it] - grow[it], 0.0)), 0.0))
    kk = each(lambda *it: _dot_nt(k[it], k[it]))
    lm = each(lambda *it: jnp.where(strict, beta[it] * kk[it] * decay[it], 0.0))

    x = each(lambda *it: eye - lm[it])
    pw = lm
    n = 2
    while n < L:
        pb = each(lambda *it: pw[it].astype(BF16))
        pw = each(lambda *it: _dot(pb[it], pb[it]))
        x = each(lambda *it: x[it] + _dot(x[it].astype(BF16), pw[it].astype(BF16)))
        n *= 2

    eg = each(lambda *it: jnp.exp(gcol[it]))
    rhs = each(lambda *it: jnp.concatenate(
        [v[it].astype(F32) * beta[it], k[it].astype(F32) * (beta[it] * eg[it])], axis=1).astype(BF16))
    sol = each(lambda *it: _dot(x[it].astype(BF16), rhs[it]))
    attn = each(lambda *it: (_dot_nt(q[it], k[it]) * decay[it]).astype(BF16))
    q_dec = each(lambda *it: (q[it].astype(F32) * eg[it]).astype(BF16))
    k_dec = each(lambda *it: (k[it].astype(F32) * jnp.exp(glast[it] - gcol[it])).astype(BF16))
    last = each(lambda *it: jnp.exp(glast[it]))
    return sol, attn, (lambda it, u: u), q_dec, k_dec, last


def _intra_chunk_pairs(qkv_ref, bg_ref, bgt_ref, *, L, cps, ns):
    P = 2 * L
    rowp = lax.broadcasted_iota(jnp.int32, (L, P), 0)
    lanep = lax.broadcasted_iota(jnp.int32, (L, P), 1)
    left = lanep < L
    colp = jnp.where(left, lanep, lanep - L)
    causal = rowp >= colp
    strict = rowp > colp
    eye = jnp.where(rowp == colp, 1.0, 0.0).astype(F32)
    r2 = lax.broadcasted_iota(jnp.int32, (P, P), 0)
    c2 = lax.broadcasted_iota(jnp.int32, (P, P), 1)
    same = (r2 < L) == (c2 < L)
    tril = jnp.where(jnp.logical_and(same, r2 >= c2), 1.0, 0.0).astype(BF16)
    triu = jnp.where(jnp.logical_and(same, r2 <= c2), 1.0, 0.0).astype(BF16)

    pairs = [(si, m) for si in range(ns) for m in range(cps // 2)]
    items = [(si, m, hd) for (si, m) in pairs for hd in range(N_HEADS)]
    each = lambda f: {it: f(*it) for it in items}
    rows = lambda m: slice(m * P, (m + 1) * P)
    side_by_side = lambda a: jnp.where(left, a[0:L], a[L:P])
    block_diag = lambda a: jnp.concatenate(
        [jnp.where(left, a, 0.0), jnp.where(left, 0.0, a)], axis=0).astype(BF16)

    bgc, gcum_c, gcum_r = {}, {}, {}
    for si, m in pairs:
        bgc[si, m] = bg_ref[si, rows(m), :]
        c1, c2_, c3 = _split3(bgc[si, m])
        gcum_c[si, m] = _dot(tril, c1) + _dot(tril, c2_) + _dot(tril, c3)
        r1, r2_, r3 = _split3(bgt_ref[si, :, rows(m)])
        gcum_r[si, m] = _dot(r1, triu) + _dot(r2_, triu) + _dot(r3, triu)

    q = each(lambda si, m, hd: qkv_ref[si, rows(m), _qkv_head(hd, 0)])
    k = each(lambda si, m, hd: qkv_ref[si, rows(m), _qkv_head(hd, 1)])
    v = each(lambda si, m, hd: qkv_ref[si, rows(m), _qkv_head(hd, 2)])
    beta = each(lambda si, m, hd: bgc[si, m][:, hd:hd + 1])
    gcol = each(lambda si, m, hd: gcum_c[si, m][:, N_HEADS + hd:N_HEADS + hd + 1])
    grow = each(lambda si, m, hd: gcum_r[si, m][N_HEADS + hd:N_HEADS + hd + 1, :])

    decay = each(lambda *it: jnp.where(
        causal, jnp.exp(jnp.where(causal, side_by_side(gcol[it]) - grow[it], 0.0)), 0.0))
    kk = each(lambda *it: side_by_side(_dot_nt(k[it], k[it])))
    lm = each(lambda *it: jnp.where(strict, side_by_side(beta[it]) * kk[it] * decay[it], 0.0))

    x = each(lambda *it: eye - lm[it])
    pw = each(lambda *it: _dot(lm[it].astype(BF16), block_diag(lm[it])))
    n = 2
    while n < L:
        w = each(lambda *it: block_diag(pw[it]))
        if 2 * n < L:
            r = each(lambda *it: _dot(jnp.concatenate([x[it], pw[it]], axis=0).astype(BF16), w[it]))
            x = each(lambda *it: x[it] + r[it][0:L])
            pw = each(lambda *it: r[it][L:P])
        else:
            x = each(lambda *it: x[it] + _dot(x[it].astype(BF16), w[it]))
        n *= 2

    eg = each(lambda *it: jnp.exp(gcol[it]))
    rhs = each(lambda *it: jnp.concatenate(
        [v[it].astype(F32) * beta[it], k[it].astype(F32) * (beta[it] * eg[it])], axis=1).astype(BF16))
    sol2 = each(lambda *it: _dot(block_diag(x[it]), rhs[it]))
    attn2 = each(lambda *it: side_by_side(_dot_nt(q[it], k[it])) * decay[it])
    q_dec2 = each(lambda *it: (q[it].astype(F32) * eg[it]).astype(BF16))

    sol, attn, q_dec, k_dec, last = {}, {}, {}, {}, {}
    for it in items:
        si, m, hd = it
        for half in range(2):
            key = (si, 2 * m + half, hd)
            hr = slice(half * L, (half + 1) * L)
            glast = gcum_c[si, m][(half + 1) * L - 1:(half + 1) * L, N_HEADS + hd:N_HEADS + hd + 1]
            sol[key] = sol2[it][hr]
            attn[key] = jnp.where(left if half == 0 else jnp.logical_not(left), attn2[it], 0.0).astype(BF16)
            q_dec[key] = q_dec2[it][hr]
            k_dec[key] = (k[it][hr].astype(F32) * jnp.exp(glast - gcol[it][hr])).astype(BF16)
            last[key] = jnp.exp(glast)

    def attn_rhs(key, u):
        z = jnp.zeros_like(u)
        return jnp.concatenate([u, z] if key[1] % 2 == 0 else [z, u], axis=0)

    return sol, attn, attn_rhs, q_dec, k_dec, last


def _delta_kernel(qkv_ref, gz_ref, bg_ref, bgt_ref, s0_ref, ghead_ref, og_ref, sout_ref, s_ref,
                  *, L, cps, ns):
    c = pl.program_id(1)
    nc = pl.num_programs(1)

    @pl.when(c == 0)
    def _():
        for si in range(ns):
            s_ref[si] = s0_ref[min(si, s0_ref.shape[0] - 1)]

    ghead = ghead_ref[...]
    rows = lambda ci: slice(ci * L, (ci + 1) * L)
    head = lambda hd: slice(hd * D_HEAD, (hd + 1) * D_HEAD)
    intra = _intra_chunk_pairs if (cps % 2 == 0 and 2 * L == LANES) else _intra_chunks
    sol, attn, attn_rhs, q_dec, k_dec, last = intra(qkv_ref, bg_ref, bgt_ref, L=L, cps=cps, ns=ns)

    state = {(si, hd): s_ref[si, hd] for si in range(ns) for hd in range(N_HEADS)}
    for ci in range(cps):
        cur = [(si, ci, hd) for si in range(ns) for hd in range(N_HEADS)]
        now = lambda f: {it: f(*it) for it in cur}
        sb = now(lambda si, ci, hd: state[si, hd].astype(BF16))
        ks = now(lambda *it: _dot(jnp.concatenate(
            [sol[it][:, D_HEAD:].astype(BF16), q_dec[it]], axis=0), sb[it]))
        ub = now(lambda *it: (sol[it][:, :D_HEAD] - ks[it][0:L]).astype(BF16))
        o = now(lambda *it: ks[it][L:2 * L] + _dot(attn[it], attn_rhs(it, ub[it])))
        for it in cur:
            si, _, hd = it
            state[si, hd] = state[si, hd] * last[it] + _dot_tn(k_dec[it], ub[it])
            on = _rms(o[it], ghead) * _silu(gz_ref[si, rows(ci), head(hd)].astype(F32))
            og_ref[si, rows(ci), head(hd)] = on.astype(BF16)
    for (si, hd), s in state.items():
        s_ref[si, hd] = s

    @pl.when(c == nc - 1)
    def _():
        sout_ref[...] = s_ref[...]


def _delta(qkv, gz, bg, bgt, s0, ghead, *, L, cps, ns):
    B, T, _ = qkv.shape
    tb = L * cps
    bs = s0.shape[0]
    assert bs == B or (bs == 1 and ns == 1)
    s_spec = (pl.BlockSpec((ns, N_HEADS, D_HEAD, D_HEAD), lambda b, c: (b, 0, 0, 0)) if bs == B else
              pl.BlockSpec((1, N_HEADS, D_HEAD, D_HEAD), lambda b, c: (0, 0, 0, 0)))
    tile = lambda n: pl.BlockSpec((ns, tb, n), lambda b, c: (b, c, 0))
    return pl.pallas_call(
        functools.partial(_delta_kernel, L=L, cps=cps, ns=ns),
        grid=(B // ns, T // tb),
        in_specs=[
            tile(D_QKV), tile(D_MODEL), tile(LANES),
            pl.BlockSpec((ns, 2 * N_HEADS, tb), lambda b, c: (b, 0, c)),
            s_spec,
            _resident((1, D_HEAD)),
        ],
        out_specs=[
            tile(D_MODEL),
            pl.BlockSpec((ns, N_HEADS, D_HEAD, D_HEAD), lambda b, c: (b, 0, 0, 0)),
        ],
        out_shape=[
            jax.ShapeDtypeStruct((B, T, D_MODEL), BF16),
            jax.ShapeDtypeStruct((B, N_HEADS, D_HEAD, D_HEAD), F32),
        ],
        scratch_shapes=[pltpu.VMEM((ns, N_HEADS, D_HEAD, D_HEAD), F32)],
        compiler_params=pltpu.CompilerParams(
            dimension_semantics=("arbitrary", "arbitrary"), vmem_limit_bytes=VMEM_LIMIT),
        name="delta_rule",
    )(qkv, gz, bg, bgt, s0, ghead)


def _mixer_out_kernel(x_ref, a_ref, og_ref, sgd_ref, wdout_ref, wo_ref, gmlp_ref, wup_ref, wdown_ref,
                      gfin_ref, y_ref):
    dn = _dot(og_ref[...], wdout_ref[...])
    mix = a_ref[...].astype(F32) + _sigmoid(sgd_ref[...].astype(F32)) * dn
    x2 = x_ref[...] + _dot(mix.astype(BF16), wo_ref[...])
    hn = _rms(x2, gmlp_ref[...]).astype(BF16)
    acc = x2
    for j in range(0, D_FF, D_MODEL):
        hm = jnp.maximum(_dot(hn, wup_ref[:, j:j + D_MODEL]), 0.0)
        acc = acc + _dot((hm * hm).astype(BF16), wdown_ref[j:j + D_MODEL, :])
    y_ref[...] = _rms(acc, gfin_ref[...])


def _mixer_out(x, a, og, sgd, p, *, tm):
    n = x.shape[0]
    tile = pl.BlockSpec((tm, D_MODEL), lambda i: (i, 0))
    return pl.pallas_call(
        _mixer_out_kernel,
        grid=(n // tm,),
        in_specs=[tile, tile, tile, tile,
                  _resident(p["wdout"].shape), _resident(p["wo"].shape), _resident((1, D_MODEL)),
                  _resident(p["wup"].shape), _resident(p["wdown"].shape), _resident((1, D_MODEL))],
        out_specs=tile,
        out_shape=jax.ShapeDtypeStruct((n, D_MODEL), F32),
        compiler_params=pltpu.CompilerParams(
            dimension_semantics=("arbitrary",), vmem_limit_bytes=VMEM_LIMIT),
        name="mixer_out",
    )(x, a, og, sgd, p["wdout"], p["wo"], p["gmlp"], p["wup"], p["wdown"], p["gfin"])


def _prep_params(g_mix, w_in, w_dw, b_dw, ln_g, ln_b, w_cout, w_short, a_log, dt_bias, g_head,
                 w_dout, w_o, g_mlp, w_up, w_down, g_final):
    w = w_in[0]
    n_main = 2 * D_MODEL + D_QKV + D_MODEL
    ba = w[:, n_main:n_main + 2 * N_HEADS]
    pad16 = lambda v: jnp.pad(v.astype(F32), (N_HEADS, 0))
    rows8 = lambda v: jnp.broadcast_to(v[:, None, :], (v.shape[0], SUBLANES, v.shape[1]))
    return dict(
        wmain=jnp.concatenate([w[:, :n_main], w[:, n_main + 2 * N_HEADS:]], axis=1).astype(BF16),
        wba=jnp.pad(ba, ((0, 0), (0, LANES - 2 * N_HEADS))).astype(BF16),
        wbat=ba.T.astype(BF16),
        gmix=g_mix[0].reshape(1, D_MODEL),
        wdw=w_dw[0], wdw8=rows8(w_dw[0]), bdw=b_dw[0].reshape(1, D_MODEL),
        lng=ln_g[0].reshape(1, D_MODEL), lnb=ln_b[0].reshape(1, D_MODEL),
        wcout=w_cout[0].astype(BF16), wsh=w_short[0], wsh8=rows8(w_short[0]),
        alog_row=jnp.pad(pad16(a_log[0]), (0, LANES - 2 * N_HEADS)).reshape(1, LANES),
        dt_row=jnp.pad(pad16(dt_bias[0]), (0, LANES - 2 * N_HEADS)).reshape(1, LANES),
        alog_col=pad16(a_log[0]).reshape(2 * N_HEADS, 1),
        dt_col=pad16(dt_bias[0]).reshape(2 * N_HEADS, 1),
        ghead=g_head[0].reshape(1, D_HEAD),
        wdout=w_dout[0].astype(BF16), wo=w_o[0].astype(BF16), gmlp=g_mlp[0].reshape(1, D_MODEL),
        wup=w_up[0].astype(BF16), wdown=w_down[0].astype(BF16), gfin=g_final.reshape(1, D_MODEL),
    )


def kernel(x_prompt, x_sample, state_conv, state_qkv_conv, state_delta, meta_tokens, g_mix, w_in, w_dw, b_dw, ln_g, ln_b, w_cout, w_short, a_log, dt_bias, g_head, w_dout, w_o, g_mlp, w_up, w_down, g_final):
    p = _prep_params(g_mix, w_in, w_dw, b_dw, ln_g, ln_b, w_cout, w_short, a_log, dt_bias, g_head,
                     w_dout, w_o, g_mlp, w_up, w_down, g_final)
    bp, seq, _ = x_prompt.shape
    bd, ts, _ = x_sample.shape

    zc = jnp.zeros((1, CONV_HALO, D_MODEL), F32)
    zq = jnp.zeros((1, SHORT_HALO, D_QKV), F32)
    zs = jnp.zeros((1, N_HEADS, D_HEAD, D_HEAD), F32)
    xm = meta_tokens.astype(F32)[None]
    _, qkv_m, gz_m, _, bg_m, bgt_m, cv_m, qk_m = _mixer_in_seq(xm, zc, zq, p, tm=N_META, row_block=N_META)
    _, s_m = _delta(qkv_m, gz_m, bg_m, bgt_m, zs, p["ghead"], L=N_META, cps=1, ns=1)

    a_p, qkv_p, gz_p, sgd_p, bg_p, bgt_p, cv_p, qk_p = _mixer_in_seq(
        x_prompt, cv_m, qk_m, p, tm=TM_IN_PROMPT, row_block=CONV_ROW_BLOCK)
    og_p, s_p = _delta(qkv_p, gz_p, bg_p, bgt_p, s_m, p["ghead"], L=DELTA_CHUNK,
                       cps=DELTA_CHUNKS_PER_STEP, ns=1)
    n_p = bp * seq
    flat = lambda v: v.reshape(n_p, D_MODEL)
    y_prompt = _mixer_out(flat(x_prompt), flat(a_p), flat(og_p), flat(sgd_p), p,
                          tm=TM_OUT_PROMPT).reshape(bp, seq, D_MODEL)

    nbuf, nsh = K_CONV - 1, K_SHORT - 1
    a_s, qkv_s, z_s, gd_s, bg_s, cv_s, qk_s = _mixer_in_dec(
        x_sample.reshape(bd, ts * D_MODEL), state_conv[0].reshape(bd, nbuf * D_MODEL),
        state_qkv_conv[0].reshape(bd, nsh * D_QKV), p, ts=ts, nb=DEC_BATCH_BLOCK)
    padt = lambda v, n: jnp.pad(v.reshape(bd, ts, n), ((0, 0), (0, DEC_PAD - ts), (0, 0)))
    bg3 = bg_s.reshape(bd, ts, LANES)
    bgt_s = jnp.pad(jnp.transpose(bg3[:, :, :2 * N_HEADS], (0, 2, 1)), ((0, 0), (0, 0), (0, DEC_PAD - ts)))
    og_s, s_s = _delta(padt(qkv_s, D_QKV), padt(z_s, D_MODEL), padt(bg_s, LANES), bgt_s, state_delta[0],
                       p["ghead"], L=DEC_PAD, cps=1, ns=DEC_SEQS_PER_STEP)
    n_s = bd * ts
    y_sample = _mixer_out(x_sample.reshape(n_s, D_MODEL), a_s.reshape(n_s, D_MODEL),
                          og_s[:, :ts].reshape(n_s, D_MODEL), gd_s.reshape(n_s, D_MODEL), p,
                          tm=TM_OUT_DECODE).reshape(bd, ts, D_MODEL)

    prompt_conv = cv_p[:, CONV_HALO - nbuf:][None]
    prompt_qkv_conv = qk_p[:, SHORT_HALO - nsh:][None]
    sample_conv = cv_s.reshape(1, bd, nbuf, D_MODEL)
    sample_qkv_conv = qk_s.reshape(1, bd, nsh, D_QKV)
    return (y_prompt, y_sample, prompt_conv, prompt_qkv_conv, s_p[None],
            sample_conv, sample_qkv_conv, s_s[None])
```

```python
import functools

import jax
import jax.numpy as jnp
from jax import lax
from jax.experimental import pallas as pl
from jax.experimental.pallas import tpu as pltpu

F32 = jnp.float32
BF16 = jnp.bfloat16

D_MODEL = 1024
N_HEADS = 8
D_HEAD = 128
D_QKV = 3 * D_MODEL
D_FF = 4 * D_MODEL
K_CONV = 31
K_SHORT = 4
N_META = 16
EPS = 1e-6

CONV_HALO = 32
SHORT_HALO = 8
LANES = 128
SUBLANES = 8
COL_BLOCK = 512
LN_ROWS = 16
ROW_STRIDE = 2
N_SLABS = D_MODEL // LANES
N_QKV_SLABS = D_QKV // LANES

OFF_GLU_A, OFF_GLU_B, OFF_QKV, OFF_Z, OFF_GC, OFF_GD = 0, 1024, 2048, 5120, 6144, 7168
ACT_GZ, ACT_SGD, ACT_A, ACT_WIDTH = D_QKV, D_QKV + D_MODEL, D_QKV + 2 * D_MODEL, D_QKV + 3 * D_MODEL

VMEM_LIMIT = 60 * 1024 * 1024

TM_IN_PROMPT = 512
TM_OUT_PROMPT = 512
TM_OUT_DECODE = 256
CONV_ROW_BLOCK = 64
DELTA_CHUNK = 64
DELTA_CHUNKS_PER_STEP = 8
DEC_PAD = 16
DEC_SEQS_PER_STEP = 4
DEC_BATCH_BLOCK = 32


def _dot(a, b):
    return jnp.dot(a, b, preferred_element_type=F32)


def _dot_nt(a, b):
    return lax.dot_general(a, b, (((1,), (1,)), ((), ())), preferred_element_type=F32)


def _dot_tn(a, b):
    return lax.dot_general(a, b, (((0,), (0,)), ((), ())), preferred_element_type=F32)


def _sigmoid(x):
    return 0.5 * jnp.tanh(0.5 * x) + 0.5


def _silu(x):
    hx = 0.5 * x
    return hx * jnp.tanh(hx) + hx


def _softplus(x):
    return jnp.maximum(x, 0.0) + jnp.log(1.0 + jnp.exp(-jnp.abs(x)))


def _rms(x, g):
    return x * lax.rsqrt(jnp.mean(x * x, axis=-1, keepdims=True) + EPS) * g


def _layer_norm_silu(c, g, b):
    mu = jnp.mean(c, axis=-1, keepdims=True)
    d = c - mu
    var = jnp.mean(d * d, axis=-1, keepdims=True)
    return _silu(d * lax.rsqrt(var + EPS) * g + b)


def _split3(x):
    x1 = x.astype(BF16)
    r1 = x - x1.astype(F32)
    x2 = r1.astype(BF16)
    x3 = (r1 - x2.astype(F32)).astype(BF16)
    return x1, x2, x3


def _resident(shape):
    nd = len(shape)
    return pl.BlockSpec(shape, lambda *_: (0,) * nd, pipeline_mode=pl.Buffered(1))


def _beta_g_rows(raw, alog_row, dt_row):
    lane = lax.broadcasted_iota(jnp.int32, raw.shape, 1)
    g = -jnp.exp(alog_row) * _softplus(raw + dt_row)
    return jnp.where(lane < N_HEADS, _sigmoid(raw), g)


def _qkv_post(acc, cb):
    y = _silu(acc)
    if cb < 2 * N_HEADS:
        y = y * lax.rsqrt(jnp.sum(y * y, axis=-1, keepdims=True) + 1e-6)
        if cb < N_HEADS:
            y = y * (D_HEAD ** -0.5)
    return y


def _interleave(main, side):
    done = 0
    for idx, task in enumerate(main):
        task()
        while done * len(main) < (idx + 1) * len(side):
            side[done]()
            done += 1
    for task in side[done:]:
        task()


def _row_groups(lo, hi):
    span = ROW_STRIDE * SUBLANES
    return [lo + m + ph for m in range(0, hi - lo, span) for ph in range(ROW_STRIDE)]


def _strided_conv(src_ref, dst_ref, w_ref, slab, first, n_taps, rows, dst_slab=None):
    ls = slice(slab * LANES, (slab + 1) * LANES)
    starts = _row_groups(*rows)
    accs = [None] * len(starts)
    for i in range(n_taps):
        w = w_ref[i, :, ls]
        for gi, s in enumerate(starts):
            term = w * src_ref[slab, pl.ds(s + first + i, SUBLANES, stride=ROW_STRIDE), :]
            accs[gi] = term if accs[gi] is None else accs[gi] + term
    dst_slab = slab if dst_slab is None else dst_slab
    for gi, s in enumerate(starts):
        dst_ref[dst_slab, pl.ds(s, SUBLANES, stride=ROW_STRIDE), :] = accs[gi]


def _mixer_in_seq_kernel(x_ref, cbuf_ref, qbuf_ref, gmix_ref, w_ref, wba_ref, wbat_ref, wdw_ref,
                         bdw_ref, lng_ref, lnb_ref, wcout_ref, wsh_ref, alr_ref, dtr_ref, alc_ref,
                         dtc_ref, act_ref, bg_ref, bgt_ref, ncv_ref, nqk_ref,
                         ext_ref, yc_ref, qext_ref, qy_ref, c_ref, *, tm, row_block):
    t = pl.program_id(1)
    nt = pl.num_programs(1)
    lanes = lambda k: slice(k * LANES, (k + 1) * LANES)

    @pl.when(t == 0)
    def _():
        for lb in range(N_SLABS):
            ext_ref[lb, 0:CONV_HALO, :] = cbuf_ref[0, :, lanes(lb)]
        for cb in range(N_QKV_SLABS):
            qext_ref[cb, 0:SHORT_HALO, :] = qbuf_ref[0, :, lanes(cb)]

    @pl.when(t > 0)
    def _():
        for lb in range(N_SLABS):
            ext_ref[lb, 0:CONV_HALO, :] = ext_ref[lb, tm:tm + CONV_HALO, :]
        for cb in range(N_QKV_SLABS):
            qext_ref[cb, 0:SHORT_HALO, :] = qext_ref[cb, tm:tm + SHORT_HALO, :]

    x = x_ref[0]
    h = _rms(x, gmix_ref[...]).astype(BF16)

    for j in range(0, D_MODEL, COL_BLOCK):
        a = _dot(h, w_ref[:, OFF_GLU_A + j:OFF_GLU_A + j + COL_BLOCK])
        b = _dot(h, w_ref[:, OFF_GLU_B + j:OFF_GLU_B + j + COL_BLOCK])
        glu = a * _sigmoid(b)
        for c in range(0, COL_BLOCK, LANES):
            ext_ref[(j + c) // LANES, CONV_HALO:CONV_HALO + tm, :] = glu[:, c:c + LANES]

    def qkv_cols(j):
        raw = _dot(h, w_ref[:, OFF_QKV + j:OFF_QKV + j + COL_BLOCK])
        for c in range(0, COL_BLOCK, LANES):
            qext_ref[(j + c) // LANES, SHORT_HALO:SHORT_HALO + tm, :] = raw[:, c:c + LANES]

    def short_conv(cb):
        buf = cb % qy_ref.shape[0]
        _strided_conv(qext_ref, qy_ref, wsh_ref, cb, SHORT_HALO - (K_SHORT - 1), K_SHORT, (0, tm),
                      dst_slab=buf)
        act_ref[0, :, lanes(cb)] = _qkv_post(qy_ref[buf], cb).astype(BF16)

    def gated(out_off, off, act, j):
        act_ref[0, :, out_off + j:out_off + j + COL_BLOCK] = act(
            _dot(h, w_ref[:, off + j:off + j + COL_BLOCK])).astype(BF16)

    def beta_g():
        bg_ref[0] = _beta_g_rows(_dot(h, wba_ref[...]), alr_ref[...], dtr_ref[...])
        rawt = _dot_nt(wbat_ref[...], h)
        srow = lax.broadcasted_iota(jnp.int32, rawt.shape, 0)
        gt = -jnp.exp(alc_ref[...]) * _softplus(rawt + dtc_ref[...])
        bgt_ref[0] = jnp.where(srow < N_HEADS, _sigmoid(rawt), gt)

    side = []
    for j in range(0, D_QKV, COL_BLOCK):
        side.append(functools.partial(qkv_cols, j))
        side += [functools.partial(short_conv, cb)
                 for cb in range(j // LANES, (j + COL_BLOCK) // LANES)]
    for j in range(0, D_MODEL, COL_BLOCK):
        side.append(functools.partial(gated, ACT_GZ, OFF_Z, _silu, j))
        side.append(functools.partial(gated, ACT_SGD, OFF_GD, _sigmoid, j))
    side.append(beta_g)

    conv = [functools.partial(_strided_conv, ext_ref, yc_ref, wdw_ref, lb, CONV_HALO - (K_CONV - 1),
                              K_CONV, (rb, rb + row_block))
            for rb in range(0, tm, row_block) for lb in range(N_SLABS)]
    _interleave(conv, side)
    for rb in range(0, tm, LN_ROWS):
        y = jnp.concatenate([yc_ref[lb, rb:rb + LN_ROWS, :] for lb in range(N_SLABS)], axis=1)
        y = y + bdw_ref[...]
        c_ref[rb:rb + LN_ROWS, :] = _layer_norm_silu(y, lng_ref[...], lnb_ref[...]).astype(BF16)

    for j in range(0, D_MODEL, COL_BLOCK):
        cs = slice(j, j + COL_BLOCK)
        conv_out = _dot(c_ref[...], wcout_ref[:, cs])
        gate_c = _dot(h, w_ref[:, OFF_GC + j:OFF_GC + j + COL_BLOCK])
        act_ref[0, :, ACT_A + j:ACT_A + j + COL_BLOCK] = (_sigmoid(gate_c) * conv_out).astype(BF16)

    @pl.when(t == nt - 1)
    def _():
        for lb in range(N_SLABS):
            ncv_ref[0, :, lanes(lb)] = ext_ref[lb, tm:tm + CONV_HALO, :]
        for cb in range(N_QKV_SLABS):
            nqk_ref[0, :, lanes(cb)] = qext_ref[cb, tm:tm + SHORT_HALO, :]


def _mixer_in_seq(x, cbuf, qbuf, p, *, tm, row_block):
    B, T, _ = x.shape
    nt = T // tm
    bs = cbuf.shape[0]
    state_idx = (lambda b, t: (b, 0, 0)) if bs == B else (lambda b, t: (0, 0, 0))
    tile = lambda n: pl.BlockSpec((1, tm, n), lambda b, t: (b, t, 0))
    in_specs = [
        tile(D_MODEL),
        pl.BlockSpec((1, CONV_HALO, D_MODEL), state_idx),
        pl.BlockSpec((1, SHORT_HALO, D_QKV), state_idx),
        _resident((1, D_MODEL)),
        _resident(p["wmain"].shape),
        _resident(p["wba"].shape),
        _resident(p["wbat"].shape),
        _resident(p["wdw8"].shape),
        _resident((1, D_MODEL)),
        _resident((1, D_MODEL)),
        _resident((1, D_MODEL)),
        _resident(p["wcout"].shape),
        _resident(p["wsh8"].shape),
        _resident((1, LANES)),
        _resident((1, LANES)),
        _resident((2 * N_HEADS, 1)),
        _resident((2 * N_HEADS, 1)),
    ]
    out_shape = [
        jax.ShapeDtypeStruct((B, T, ACT_WIDTH), BF16),
        jax.ShapeDtypeStruct((B, T, LANES), F32),
        jax.ShapeDtypeStruct((B, 2 * N_HEADS, T), F32),
        jax.ShapeDtypeStruct((B, CONV_HALO, D_MODEL), F32),
        jax.ShapeDtypeStruct((B, SHORT_HALO, D_QKV), F32),
    ]
    out_specs = [
        tile(ACT_WIDTH), tile(LANES),
        pl.BlockSpec((1, 2 * N_HEADS, tm), lambda b, t: (b, 0, t)),
        pl.BlockSpec((1, CONV_HALO, D_MODEL), lambda b, t: (b, 0, 0)),
        pl.BlockSpec((1, SHORT_HALO, D_QKV), lambda b, t: (b, 0, 0)),
    ]
    return pl.pallas_call(
        functools.partial(_mixer_in_seq_kernel, tm=tm, row_block=row_block),
        grid=(B, nt),
        in_specs=in_specs,
        out_specs=out_specs,
        out_shape=out_shape,
        scratch_shapes=[
            pltpu.VMEM((N_SLABS, tm + CONV_HALO, LANES), F32),
            pltpu.VMEM((N_SLABS, tm, LANES), F32),
            pltpu.VMEM((N_QKV_SLABS, tm + SHORT_HALO, LANES), F32),
            pltpu.VMEM((2, tm, LANES), F32),
            pltpu.VMEM((tm, D_MODEL), BF16),
        ],
        compiler_params=pltpu.CompilerParams(
            dimension_semantics=("arbitrary", "arbitrary"), vmem_limit_bytes=VMEM_LIMIT),
        name="mixer_in_seq",
    )(x, cbuf, qbuf, p["gmix"], p["wmain"], p["wba"], p["wbat"], p["wdw8"], p["bdw"], p["lng"],
      p["lnb"], p["wcout"], p["wsh8"], p["alog_row"], p["dt_row"], p["alog_col"], p["dt_col"])


def _mixer_in_dec_kernel(x_ref, cbuf_ref, qbuf_ref, gmix_ref, w_ref, wba_ref, wdw_ref, bdw_ref,
                         lng_ref, lnb_ref, wcout_ref, wsh_ref, alr_ref, dtr_ref,
                         a_ref, qkv_ref, gz_ref, sgd_ref, bg_ref, glu_ref, raw_ref, c_ref, *, ts, nb):
    n = ts * nb
    x = x_ref[...].reshape(n, D_MODEL)
    h = _rms(x, gmix_ref[...]).astype(BF16)

    for j in range(0, D_MODEL, COL_BLOCK):
        a = _dot(h, w_ref[:, OFF_GLU_A + j:OFF_GLU_A + j + COL_BLOCK])
        b = _dot(h, w_ref[:, OFF_GLU_B + j:OFF_GLU_B + j + COL_BLOCK])
        glu_ref[:, :, j:j + COL_BLOCK] = (a * _sigmoid(b)).reshape(ts, nb, COL_BLOCK)

    nbuf = K_CONV - 1
    for t in range(ts):
        acc = jnp.zeros((nb, D_MODEL), F32) + bdw_ref[...]
        for i in range(K_CONV):
            j = t + i
            row = cbuf_ref[j] if j < nbuf else glu_ref[j - nbuf]
            acc = acc + wdw_ref[i:i + 1, :] * row
        c_ref[t * nb:(t + 1) * nb, :] = _layer_norm_silu(acc, lng_ref[...], lnb_ref[...]).astype(BF16)

    conv_out = _dot(c_ref[...], wcout_ref[...])
    gate_c = _dot(h, w_ref[:, OFF_GC:OFF_GC + D_MODEL])
    a_ref[...] = (_sigmoid(gate_c) * conv_out).astype(BF16).reshape(ts, nb, D_MODEL)

    for j in range(0, D_QKV, COL_BLOCK):
        raw_ref[:, :, j:j + COL_BLOCK] = _dot(
            h, w_ref[:, OFF_QKV + j:OFF_QKV + j + COL_BLOCK]).reshape(ts, nb, COL_BLOCK)
    nsh = K_SHORT - 1
    for t in range(ts):
        for cb in range(D_QKV // LANES):
            sl = slice(cb * LANES, (cb + 1) * LANES)
            acc = None
            for i in range(K_SHORT):
                j = t + i
                row = qbuf_ref[j, :, sl] if j < nsh else raw_ref[j - nsh, :, sl]
                term = wsh_ref[i:i + 1, sl] * row
                acc = term if acc is None else acc + term
            qkv_ref[t, :, sl] = _qkv_post(acc, cb).astype(BF16)

    gz_ref[...] = _silu(_dot(h, w_ref[:, OFF_Z:OFF_Z + D_MODEL])).astype(BF16).reshape(ts, nb, D_MODEL)
    sgd_ref[...] = _sigmoid(_dot(h, w_ref[:, OFF_GD:OFF_GD + D_MODEL])).astype(BF16).reshape(
        ts, nb, D_MODEL)
    bg_ref[...] = _beta_g_rows(_dot(h, wba_ref[...]), alr_ref[...], dtr_ref[...]).reshape(ts, nb, LANES)


def _mixer_in_dec(xt, cbuft, qbuft, p, *, nb):
    ts, B, _ = xt.shape
    blk = lambda r, n: pl.BlockSpec((r, nb, n), lambda i: (0, i, 0))
    in_specs = [
        blk(ts, D_MODEL), blk(K_CONV - 1, D_MODEL), blk(K_SHORT - 1, D_QKV),
        _resident((1, D_MODEL)),
        _resident(p["wmain"].shape),
        _resident(p["wba"].shape),
        _resident(p["wdw"].shape),
        _resident((1, D_MODEL)),
        _resident((1, D_MODEL)),
        _resident((1, D_MODEL)),
        _resident(p["wcout"].shape),
        _resident(p["wsh"].shape),
        _resident((1, LANES)),
        _resident((1, LANES)),
    ]
    out_shape = [
        jax.ShapeDtypeStruct((ts, B, D_MODEL), BF16),
        jax.ShapeDtypeStruct((ts, B, D_QKV), BF16),
        jax.ShapeDtypeStruct((ts, B, D_MODEL), BF16),
        jax.ShapeDtypeStruct((ts, B, D_MODEL), BF16),
        jax.ShapeDtypeStruct((ts, B, LANES), F32),
        jax.ShapeDtypeStruct((ts, B, D_MODEL), F32),
        jax.ShapeDtypeStruct((ts, B, D_QKV), F32),
    ]
    out_specs = [blk(ts, D_MODEL), blk(ts, D_QKV), blk(ts, D_MODEL), blk(ts, D_MODEL),
                 blk(ts, LANES), blk(ts, D_MODEL), blk(ts, D_QKV)]
    return pl.pallas_call(
        functools.partial(_mixer_in_dec_kernel, ts=ts, nb=nb),
        grid=(B // nb,),
        in_specs=in_specs,
        out_specs=out_specs,
        out_shape=out_shape,
        scratch_shapes=[pltpu.VMEM((ts * nb, D_MODEL), BF16)],
        compiler_params=pltpu.CompilerParams(
            dimension_semantics=("arbitrary",), vmem_limit_bytes=VMEM_LIMIT),
        name="mixer_in_dec",
    )(xt, cbuft, qbuft, p["gmix"], p["wmain"], p["wba"], p["wdw"], p["bdw"], p["lng"], p["lnb"],
      p["wcout"], p["wsh"], p["alog_row"], p["dt_row"])


def _qkv_head(hd, part):
    return slice(part * D_MODEL + hd * D_HEAD, part * D_MODEL + (hd + 1) * D_HEAD)


def _intra_chunks(qkv_ref, bg_ref, bgt_ref, *, L, cps, ns):
    row = lax.broadcasted_iota(jnp.int32, (L, L), 0)
    col = lax.broadcasted_iota(jnp.int32, (L, L), 1)
    causal = row >= col
    strict = row > col
    tril = jnp.where(causal, 1.0, 0.0).astype(BF16)
    triu = jnp.where(row <= col, 1.0, 0.0).astype(BF16)
    eye = jnp.where(row == col, 1.0, 0.0).astype(F32)

    blocks = [(si, ci) for si in range(ns) for ci in range(cps)]
    items = [(si, ci, hd) for (si, ci) in blocks for hd in range(N_HEADS)]
    each = lambda f: {it: f(*it) for it in items}
    rows = lambda ci: slice(ci * L, (ci + 1) * L)

    bgc, gcum_c, gcum_r = {}, {}, {}
    for si, ci in blocks:
        bgc[si, ci] = bg_ref[si, rows(ci), :]
        c1, c2, c3 = _split3(bgc[si, ci])
        gcum_c[si, ci] = _dot(tril, c1) + _dot(tril, c2) + _dot(tril, c3)
        r1, r2, r3 = _split3(bgt_ref[si, :, rows(ci)])
        gcum_r[si, ci] = _dot(r1, triu) + _dot(r2, triu) + _dot(r3, triu)

    q = each(lambda si, ci, hd: qkv_ref[si, rows(ci), _qkv_head(hd, 0)])
    k = each(lambda si, ci, hd: qkv_ref[si, rows(ci), _qkv_head(hd, 1)])
    v = each(lambda si, ci, hd: qkv_ref[si, rows(ci), _qkv_head(hd, 2)])
    beta = each(lambda si, ci, hd: bgc[si, ci][:, hd:hd + 1])
    gcol = each(lambda si, ci, hd: gcum_c[si, ci][:, N_HEADS + hd:N_HEADS + hd + 1])
    grow = each(lambda si, ci, hd: gcum_r[si, ci][N_HEADS + hd:N_HEADS + hd + 1, :])
    glast = each(lambda si, ci, hd: gcum_c[si, ci][L - 1:L, N_HEADS + hd:N_HEADS + hd + 1])

    decay = each(lambda *it: jnp.where(
        causal, jnp.exp(jnp.where(causal, gcol[it] - grow[it], 0.0)), 0.0))
    kk = each(lambda *it: _dot_nt(k[it], k[it]))
    lm = each(lambda *it: jnp.where(strict, beta[it] * kk[it] * decay[it], 0.0))

    x = each(lambda *it: eye - lm[it])
    pw = lm
    n = 2
    while n < L:
        pb = each(lambda *it: pw[it].astype(BF16))
        pw = each(lambda *it: _dot(pb[it], pb[it]))
        x = each(lambda *it: x[it] + _dot(x[it].astype(BF16), pw[it].astype(BF16)))
        n *= 2

    eg = each(lambda *it: jnp.exp(gcol[it]))
    rhs = each(lambda *it: jnp.concatenate(
        [v[it].astype(F32) * beta[it], k[it].astype(F32) * (beta[it] * eg[it])], axis=1).astype(BF16))
    sol = each(lambda *it: _dot(x[it].astype(BF16), rhs[it]))
    attn = each(lambda *it: (_dot_nt(q[it], k[it]) * decay[it]).astype(BF16))
    q_dec = each(lambda *it: (q[it].astype(F32) * eg[it]).astype(BF16))
    k_dec = each(lambda *it: (k[it].astype(F32) * jnp.exp(glast[it] - gcol[it])).astype(BF16))
    last = each(lambda *it: jnp.exp(glast[it]))
    return sol, attn, (lambda it, u: u), q_dec, k_dec, last


def _intra_chunk_pairs(qkv_ref, bg_ref, bgt_ref, *, L, cps, ns):
    P = 2 * L
    rowp = lax.broadcasted_iota(jnp.int32, (L, P), 0)
    lanep = lax.broadcasted_iota(jnp.int32, (L, P), 1)
    left = lanep < L
    colp = jnp.where(left, lanep, lanep - L)
    causal = rowp >= colp
    strict = rowp > colp
    eye = jnp.where(rowp == colp, 1.0, 0.0).astype(F32)
    r2 = lax.broadcasted_iota(jnp.int32, (P, P), 0)
    c2 = lax.broadcasted_iota(jnp.int32, (P, P), 1)
    same = (r2 < L) == (c2 < L)
    tril = jnp.where(jnp.logical_and(same, r2 >= c2), 1.0, 0.0).astype(BF16)
    triu = jnp.where(jnp.logical_and(same, r2 <= c2), 1.0, 0.0).astype(BF16)

    pairs = [(si, m) for si in range(ns) for m in range(cps // 2)]
    items = [(si, m, hd) for (si, m) in pairs for hd in range(N_HEADS)]
    each = lambda f: {it: f(*it) for it in items}
    rows = lambda m: slice(m * P, (m + 1) * P)
    side_by_side = lambda a: jnp.where(left, a[0:L], a[L:P])
    block_diag = lambda a: jnp.concatenate(
        [jnp.where(left, a, 0.0), jnp.where(left, 0.0, a)], axis=0).astype(BF16)

    bgc, gcum_c, gcum_r = {}, {}, {}
    for si, m in pairs:
        bgc[si, m] = bg_ref[si, rows(m), :]
        c1, c2_, c3 = _split3(bgc[si, m])
        gcum_c[si, m] = _dot(tril, c1) + _dot(tril, c2_) + _dot(tril, c3)
        r1, r2_, r3 = _split3(bgt_ref[si, :, rows(m)])
        gcum_r[si, m] = _dot(r1, triu) + _dot(r2_, triu) + _dot(r3, triu)

    q = each(lambda si, m, hd: qkv_ref[si, rows(m), _qkv_head(hd, 0)])
    k = each(lambda si, m, hd: qkv_ref[si, rows(m), _qkv_head(hd, 1)])
    v = each(lambda si, m, hd: qkv_ref[si, rows(m), _qkv_head(hd, 2)])
    beta = each(lambda si, m, hd: bgc[si, m][:, hd:hd + 1])
    gcol = each(lambda si, m, hd: gcum_c[si, m][:, N_HEADS + hd:N_HEADS + hd + 1])
    grow = each(lambda si, m, hd: gcum_r[si, m][N_HEADS + hd:N_HEADS + hd + 1, :])

    decay = each(lambda *it: jnp.where(
        causal, jnp.exp(jnp.where(causal, side_by_side(gcol[it]) - grow[it], 0.0)), 0.0))
    kk = each(lambda *it: side_by_side(_dot_nt(k[it], k[it])))
    lm = each(lambda *it: jnp.where(strict, side_by_side(beta[it]) * kk[it] * decay[it], 0.0))

    x = each(lambda *it: eye - lm[it])
    pw = each(lambda *it: _dot(lm[it].astype(BF16), block_diag(lm[it])))
    n = 2
    while n < L:
        w = each(lambda *it: block_diag(pw[it]))
        if 2 * n < L:
            r = each(lambda *it: _dot(jnp.concatenate([x[it], pw[it]], axis=0).astype(BF16), w[it]))
            x = each(lambda *it: x[it] + r[it][0:L])
            pw = each(lambda *it: r[it][L:P])
        else:
            x = each(lambda *it: x[it] + _dot(x[it].astype(BF16), w[it]))
        n *= 2

    eg = each(lambda *it: jnp.exp(gcol[it]))
    rhs = each(lambda *it: jnp.concatenate(
        [v[it].astype(F32) * beta[it], k[it].astype(F32) * (beta[it] * eg[it])], axis=1).astype(BF16))
    sol2 = each(lambda *it: _dot(block_diag(x[it]), rhs[it]))
    attn2 = each(lambda *it: side_by_side(_dot_nt(q[it], k[it])) * decay[it])
    q_dec2 = each(lambda *it: (q[it].astype(F32) * eg[it]).astype(BF16))

    sol, attn, q_dec, k_dec, last = {}, {}, {}, {}, {}
    for it in items:
        si, m, hd = it
        for half in range(2):
            key = (si, 2 * m + half, hd)
            hr = slice(half * L, (half + 1) * L)
            glast = gcum_c[si, m][(half + 1) * L - 1:(half + 1) * L, N_HEADS + hd:N_HEADS + hd + 1]
            sol[key] = sol2[it][hr]
            attn[key] = jnp.where(left if half == 0 else jnp.logical_not(left), attn2[it], 0.0).astype(BF16)
            q_dec[key] = q_dec2[it][hr]
            k_dec[key] = (k[it][hr].astype(F32) * jnp.exp(glast - gcol[it][hr])).astype(BF16)
            last[key] = jnp.exp(glast)

    def attn_rhs(key, u):
        z = jnp.zeros_like(u)
        return jnp.concatenate([u, z] if key[1] % 2 == 0 else [z, u], axis=0)

    return sol, attn, attn_rhs, q_dec, k_dec, last


def _delta_kernel(qkv_ref, gz_ref, bg_ref, bgt_ref, s0_ref, ghead_ref, og_ref, sout_ref, s_ref,
                  *, L, cps, ns):
    c = pl.program_id(1)
    nc = pl.num_programs(1)

    @pl.when(c == 0)
    def _():
        for si in range(ns):
            s_ref[si] = s0_ref[min(si, s0_ref.shape[0] - 1)]

    ghead = ghead_ref[...]
    rows = lambda ci: slice(ci * L, (ci + 1) * L)
    head = lambda hd: slice(hd * D_HEAD, (hd + 1) * D_HEAD)
    intra = _intra_chunk_pairs if (cps % 2 == 0 and 2 * L == LANES) else _intra_chunks
    sol, attn, attn_rhs, q_dec, k_dec, last = intra(qkv_ref, bg_ref, bgt_ref, L=L, cps=cps, ns=ns)

    state = {(si, hd): s_ref[si, hd] for si in range(ns) for hd in range(N_HEADS)}
    for ci in range(cps):
        cur = [(si, ci, hd) for si in range(ns) for hd in range(N_HEADS)]
        now = lambda f: {it: f(*it) for it in cur}
        sb = now(lambda si, ci, hd: state[si, hd].astype(BF16))
        ks = now(lambda *it: _dot(jnp.concatenate(
            [sol[it][:, D_HEAD:].astype(BF16), q_dec[it]], axis=0), sb[it]))
        ub = now(lambda *it: (sol[it][:, :D_HEAD] - ks[it][0:L]).astype(BF16))
        o = now(lambda *it: ks[it][L:2 * L] + _dot(attn[it], attn_rhs(it, ub[it])))
        for it in cur:
            si, _, hd = it
            state[si, hd] = state[si, hd] * last[it] + _dot_tn(k_dec[it], ub[it])
            on = _rms(o[it], ghead) * gz_ref[si, rows(ci), head(hd)].astype(F32)
            og_ref[si, rows(ci), head(hd)] = on.astype(BF16)
    for (si, hd), s in state.items():
        s_ref[si, hd] = s

    @pl.when(c == nc - 1)
    def _():
        sout_ref[...] = s_ref[...]


def _delta(qkv, gz, bg, bgt, s0, ghead, *, L, cps, ns, gz_col=0):
    B, T, _ = qkv.shape
    tb = L * cps
    bs = s0.shape[0]
    assert bs == B or (bs == 1 and ns == 1)
    s_spec = (pl.BlockSpec((ns, N_HEADS, D_HEAD, D_HEAD), lambda b, c: (b, 0, 0, 0)) if bs == B else
              pl.BlockSpec((1, N_HEADS, D_HEAD, D_HEAD), lambda b, c: (0, 0, 0, 0)))
    tile = lambda n, col=0: pl.BlockSpec((ns, tb, n), lambda b, c: (b, c, col))
    return pl.pallas_call(
        functools.partial(_delta_kernel, L=L, cps=cps, ns=ns),
        grid=(B // ns, T // tb),
        in_specs=[
            tile(D_QKV), tile(D_MODEL, gz_col), tile(LANES),
            pl.BlockSpec((ns, 2 * N_HEADS, tb), lambda b, c: (b, 0, c)),
            s_spec,
            _resident((1, D_HEAD)),
        ],
        out_specs=[
            tile(D_MODEL),
            pl.BlockSpec((ns, N_HEADS, D_HEAD, D_HEAD), lambda b, c: (b, 0, 0, 0)),
        ],
        out_shape=[
            jax.ShapeDtypeStruct((B, T, D_MODEL), BF16),
            jax.ShapeDtypeStruct((B, N_HEADS, D_HEAD, D_HEAD), F32),
        ],
        scratch_shapes=[pltpu.VMEM((ns, N_HEADS, D_HEAD, D_HEAD), F32)],
        compiler_params=pltpu.CompilerParams(
            dimension_semantics=("arbitrary", "arbitrary"), vmem_limit_bytes=VMEM_LIMIT),
        name="delta_rule",
    )(qkv, gz, bg, bgt, s0, ghead)


def _mixer_out_kernel(x_ref, a_ref, og_ref, sgd_ref, wdout_ref, wo_ref, gmlp_ref, wup_ref, wdown_ref,
                      gfin_ref, y_ref):
    dn = _dot(og_ref[...], wdout_ref[...])
    mix = a_ref[...].astype(F32) + sgd_ref[...].astype(F32) * dn
    x2 = x_ref[...] + _dot(mix.astype(BF16), wo_ref[...])
    hn = _rms(x2, gmlp_ref[...]).astype(BF16)
    acc = x2
    for j in range(0, D_FF, D_MODEL):
        hm = jnp.maximum(_dot(hn, wup_ref[:, j:j + D_MODEL]), 0.0)
        acc = acc + _dot((hm * hm).astype(BF16), wdown_ref[j:j + D_MODEL, :])
    y_ref[...] = _rms(acc, gfin_ref[...])


def _mixer_out(x, a, og, sgd, p, *, tm, a_col=0, sgd_col=0):
    n = x.shape[0]
    tile = pl.BlockSpec((tm, D_MODEL), lambda i: (i, 0))
    col_tile = lambda col: pl.BlockSpec((tm, D_MODEL), lambda i: (i, col))
    return pl.pallas_call(
        _mixer_out_kernel,
        grid=(n // tm,),
        in_specs=[tile, col_tile(a_col), tile, col_tile(sgd_col),
                  _resident(p["wdout"].shape), _resident(p["wo"].shape), _resident((1, D_MODEL)),
                  _resident(p["wup"].shape), _resident(p["wdown"].shape), _resident((1, D_MODEL))],
        out_specs=tile,
        out_shape=jax.ShapeDtypeStruct((n, D_MODEL), F32),
        compiler_params=pltpu.CompilerParams(
            dimension_semantics=("arbitrary",), vmem_limit_bytes=VMEM_LIMIT),
        name="mixer_out",
    )(x, a, og, sgd, p["wdout"], p["wo"], p["gmlp"], p["wup"], p["wdown"], p["gfin"])


def _prep_params(g_mix, w_in, w_dw, b_dw, ln_g, ln_b, w_cout, w_short, a_log, dt_bias, g_head,
                 w_dout, w_o, g_mlp, w_up, w_down, g_final):
    w = w_in[0]
    n_main = 2 * D_MODEL + D_QKV + D_MODEL
    ba = w[:, n_main:n_main + 2 * N_HEADS]
    pad16 = lambda v: jnp.pad(v.astype(F32), (N_HEADS, 0))
    rows8 = lambda v: jnp.broadcast_to(v[:, None, :], (v.shape[0], SUBLANES, v.shape[1]))
    return dict(
        wmain=jnp.concatenate([w[:, :n_main], w[:, n_main + 2 * N_HEADS:]], axis=1).astype(BF16),
        wba=jnp.pad(ba, ((0, 0), (0, LANES - 2 * N_HEADS))).astype(BF16),
        wbat=ba.T.astype(BF16),
        gmix=g_mix[0].reshape(1, D_MODEL),
        wdw=w_dw[0], wdw8=rows8(w_dw[0]), bdw=b_dw[0].reshape(1, D_MODEL),
        lng=ln_g[0].reshape(1, D_MODEL), lnb=ln_b[0].reshape(1, D_MODEL),
        wcout=w_cout[0].astype(BF16), wsh=w_short[0], wsh8=rows8(w_short[0]),
        alog_row=jnp.pad(pad16(a_log[0]), (0, LANES - 2 * N_HEADS)).reshape(1, LANES),
        dt_row=jnp.pad(pad16(dt_bias[0]), (0, LANES - 2 * N_HEADS)).reshape(1, LANES),
        alog_col=pad16(a_log[0]).reshape(2 * N_HEADS, 1),
        dt_col=pad16(dt_bias[0]).reshape(2 * N_HEADS, 1),
        ghead=g_head[0].reshape(1, D_HEAD),
        wdout=w_dout[0].astype(BF16), wo=w_o[0].astype(BF16), gmlp=g_mlp[0].reshape(1, D_MODEL),
        wup=w_up[0].astype(BF16), wdown=w_down[0].astype(BF16), gfin=g_final.reshape(1, D_MODEL),
    )


def kernel(x_prompt, x_sample, state_conv, state_qkv_conv, state_delta, meta_tokens, g_mix, w_in, w_dw, b_dw, ln_g, ln_b, w_cout, w_short, a_log, dt_bias, g_head, w_dout, w_o, g_mlp, w_up, w_down, g_final):
    p = _prep_params(g_mix, w_in, w_dw, b_dw, ln_g, ln_b, w_cout, w_short, a_log, dt_bias, g_head,
                     w_dout, w_o, g_mlp, w_up, w_down, g_final)
    bp, seq, _ = x_prompt.shape
    bd, ts, _ = x_sample.shape

    zc = jnp.zeros((1, CONV_HALO, D_MODEL), F32)
    zq = jnp.zeros((1, SHORT_HALO, D_QKV), F32)
    zs = jnp.zeros((1, N_HEADS, D_HEAD, D_HEAD), F32)
    xm = meta_tokens.astype(F32)[None]
    gz_col, sgd_col, a_col = ACT_GZ // D_MODEL, ACT_SGD // D_MODEL, ACT_A // D_MODEL
    act_m, bg_m, bgt_m, cv_m, qk_m = _mixer_in_seq(xm, zc, zq, p, tm=N_META, row_block=N_META)
    _, s_m = _delta(act_m, act_m, bg_m, bgt_m, zs, p["ghead"], L=N_META, cps=1, ns=1, gz_col=gz_col)

    act_p, bg_p, bgt_p, cv_p, qk_p = _mixer_in_seq(
        x_prompt, cv_m, qk_m, p, tm=TM_IN_PROMPT, row_block=CONV_ROW_BLOCK)
    og_p, s_p = _delta(act_p, act_p, bg_p, bgt_p, s_m, p["ghead"], L=DELTA_CHUNK,
                       cps=DELTA_CHUNKS_PER_STEP, ns=1, gz_col=gz_col)
    n_p = bp * seq
    act_flat = act_p.reshape(n_p, ACT_WIDTH)
    y_prompt = _mixer_out(x_prompt.reshape(n_p, D_MODEL), act_flat, og_p.reshape(n_p, D_MODEL), act_flat,
                          p, tm=TM_OUT_PROMPT, a_col=a_col, sgd_col=sgd_col).reshape(bp, seq, D_MODEL)

    tmaj = lambda v: jnp.transpose(v, (1, 0, 2))
    a_s, qkv_s, gz_s, sgd_s, bg_s, glu_s, raw_s = _mixer_in_dec(
        tmaj(x_sample), tmaj(state_conv[0]), tmaj(state_qkv_conv[0]), p, nb=DEC_BATCH_BLOCK)
    padt = lambda v: jnp.pad(tmaj(v), ((0, 0), (0, DEC_PAD - ts), (0, 0)))
    bgt_s = jnp.pad(jnp.transpose(bg_s[:, :, :2 * N_HEADS], (1, 2, 0)), ((0, 0), (0, 0), (0, DEC_PAD - ts)))
    og_s, s_s = _delta(padt(qkv_s), padt(gz_s), padt(bg_s), bgt_s, state_delta[0], p["ghead"],
                       L=DEC_PAD, cps=1, ns=DEC_SEQS_PER_STEP)
    n_s = bd * ts
    y_sample = _mixer_out(x_sample.reshape(n_s, D_MODEL), tmaj(a_s).reshape(n_s, D_MODEL),
                          og_s[:, :ts].reshape(n_s, D_MODEL), tmaj(sgd_s).reshape(n_s, D_MODEL), p,
                          tm=TM_OUT_DECODE).reshape(bd, ts, D_MODEL)

    nbuf, nsh = K_CONV - 1, K_SHORT - 1
    prompt_conv = cv_p[:, CONV_HALO - nbuf:][None]
    prompt_qkv_conv = qk_p[:, SHORT_HALO - nsh:][None]
    sample_conv = jnp.concatenate([state_conv[0][:, ts:], tmaj(glu_s)], axis=1)[None]
    sample_qkv_conv = tmaj(raw_s[ts - nsh:])[None]
    return (y_prompt, y_sample, prompt_conv, prompt_qkv_conv, s_p[None],
            sample_conv, sample_qkv_conv, s_s[None])
```

```python
import functools

import jax
import jax.numpy as jnp
from jax import lax
from jax.experimental import pallas as pl
from jax.experimental.pallas import tpu as pltpu

F32 = jnp.float32
BF16 = jnp.bfloat16

D_MODEL = 1024
N_HEADS = 8
D_HEAD = 128
D_QKV = 3 * D_MODEL
D_FF = 4 * D_MODEL
K_CONV = 31
K_SHORT = 4
N_META = 16
EPS = 1e-6

CONV_HALO = 32
SHORT_HALO = 8
LANES = 128
SUBLANES = 8
COL_BLOCK = 512
LN_ROWS = 16
ROW_STRIDE = 2
N_SLABS = D_MODEL // LANES
N_QKV_SLABS = D_QKV // LANES

OFF_GLU_A, OFF_GLU_B, OFF_QKV, OFF_Z, OFF_GC, OFF_GD = 0, 1024, 2048, 5120, 6144, 7168
ACT_GZ, ACT_SGD, ACT_A, ACT_WIDTH = D_QKV, D_QKV + D_MODEL, D_QKV + 2 * D_MODEL, D_QKV + 3 * D_MODEL

VMEM_LIMIT = 60 * 1024 * 1024

TM_IN_PROMPT = 512
TM_OUT_PROMPT = 512
TM_OUT_DECODE = 256
CONV_ROW_BLOCK = 64
DELTA_CHUNK = 64
DELTA_CHUNKS_PER_STEP = 8
DEC_PAD = 16
DEC_SEQS_PER_STEP = 8
DEC_BATCH_BLOCK = 64


def _dot(a, b):
    return jnp.dot(a, b, preferred_element_type=F32)


def _dot_nt(a, b):
    return lax.dot_general(a, b, (((1,), (1,)), ((), ())), preferred_element_type=F32)


def _dot_tn(a, b):
    return lax.dot_general(a, b, (((0,), (0,)), ((), ())), preferred_element_type=F32)


def _sigmoid(x):
    return 0.5 * jnp.tanh(0.5 * x) + 0.5


def _silu(x):
    hx = 0.5 * x
    return hx * jnp.tanh(hx) + hx


def _softplus(x):
    return jnp.maximum(x, 0.0) + jnp.log(1.0 + jnp.exp(-jnp.abs(x)))


def _rms(x, g):
    return x * lax.rsqrt(jnp.mean(x * x, axis=-1, keepdims=True) + EPS) * g


def _layer_norm_silu(c, g, b):
    mu = jnp.mean(c, axis=-1, keepdims=True)
    d = c - mu
    var = jnp.mean(d * d, axis=-1, keepdims=True)
    return _silu(d * lax.rsqrt(var + EPS) * g + b)


def _split3(x):
    x1 = x.astype(BF16)
    r1 = x - x1.astype(F32)
    x2 = r1.astype(BF16)
    x3 = (r1 - x2.astype(F32)).astype(BF16)
    return x1, x2, x3


def _resident(shape):
    nd = len(shape)
    return pl.BlockSpec(shape, lambda *_: (0,) * nd, pipeline_mode=pl.Buffered(1))


def _beta_g_rows(raw, alog_row, dt_row):
    lane = lax.broadcasted_iota(jnp.int32, raw.shape, 1)
    g = -jnp.exp(alog_row) * _softplus(raw + dt_row)
    return jnp.where(lane < N_HEADS, _sigmoid(raw), g)


def _qkv_post(acc, cb):
    y = _silu(acc)
    if cb < 2 * N_HEADS:
        y = y * lax.rsqrt(jnp.sum(y * y, axis=-1, keepdims=True) + 1e-6)
        if cb < N_HEADS:
            y = y * (D_HEAD ** -0.5)
    return y


def _interleave(main, side):
    done = 0
    for idx, task in enumerate(main):
        task()
        while done * len(main) < (idx + 1) * len(side):
            side[done]()
            done += 1
    for task in side[done:]:
        task()


def _row_groups(lo, hi):
    span = ROW_STRIDE * SUBLANES
    return [lo + m + ph for m in range(0, hi - lo, span) for ph in range(ROW_STRIDE)]


def _strided_conv(src_ref, dst_ref, w_ref, slab, first, n_taps, rows, dst_slab=None):
    ls = slice(slab * LANES, (slab + 1) * LANES)
    starts = _row_groups(*rows)
    accs = [None] * len(starts)
    for i in range(n_taps):
        w = w_ref[i, :, ls]
        for gi, s in enumerate(starts):
            term = w * src_ref[slab, pl.ds(s + first + i, SUBLANES, stride=ROW_STRIDE), :]
            accs[gi] = term if accs[gi] is None else accs[gi] + term
    dst_slab = slab if dst_slab is None else dst_slab
    for gi, s in enumerate(starts):
        dst_ref[dst_slab, pl.ds(s, SUBLANES, stride=ROW_STRIDE), :] = accs[gi]


def _mixer_in_seq_kernel(x_ref, cbuf_ref, qbuf_ref, gmix_ref, w_ref, wba_ref, wbat_ref, wdw_ref,
                         bdw_ref, lng_ref, lnb_ref, wcout_ref, wsh_ref, alr_ref, dtr_ref, alc_ref,
                         dtc_ref, act_ref, bg_ref, bgt_ref, ncv_ref, nqk_ref,
                         ext_ref, yc_ref, qext_ref, qy_ref, c_ref, *, tm, row_block):
    t = pl.program_id(1)
    nt = pl.num_programs(1)
    lanes = lambda k: slice(k * LANES, (k + 1) * LANES)

    @pl.when(t == 0)
    def _():
        for lb in range(N_SLABS):
            ext_ref[lb, 0:CONV_HALO, :] = cbuf_ref[0, :, lanes(lb)]
        for cb in range(N_QKV_SLABS):
            qext_ref[cb, 0:SHORT_HALO, :] = qbuf_ref[0, :, lanes(cb)]

    @pl.when(t > 0)
    def _():
        for lb in range(N_SLABS):
            ext_ref[lb, 0:CONV_HALO, :] = ext_ref[lb, tm:tm + CONV_HALO, :]
        for cb in range(N_QKV_SLABS):
            qext_ref[cb, 0:SHORT_HALO, :] = qext_ref[cb, tm:tm + SHORT_HALO, :]

    x = x_ref[0]
    h = _rms(x, gmix_ref[...]).astype(BF16)

    for j in range(0, D_MODEL, COL_BLOCK):
        a = _dot(h, w_ref[:, OFF_GLU_A + j:OFF_GLU_A + j + COL_BLOCK])
        b = _dot(h, w_ref[:, OFF_GLU_B + j:OFF_GLU_B + j + COL_BLOCK])
        glu = a * _sigmoid(b)
        for c in range(0, COL_BLOCK, LANES):
            ext_ref[(j + c) // LANES, CONV_HALO:CONV_HALO + tm, :] = glu[:, c:c + LANES]

    def qkv_cols(j):
        raw = _dot(h, w_ref[:, OFF_QKV + j:OFF_QKV + j + COL_BLOCK])
        for c in range(0, COL_BLOCK, LANES):
            qext_ref[(j + c) // LANES, SHORT_HALO:SHORT_HALO + tm, :] = raw[:, c:c + LANES]

    def short_conv(cb):
        buf = cb % qy_ref.shape[0]
        _strided_conv(qext_ref, qy_ref, wsh_ref, cb, SHORT_HALO - (K_SHORT - 1), K_SHORT, (0, tm),
                      dst_slab=buf)
        act_ref[0, :, lanes(cb)] = _qkv_post(qy_ref[buf], cb).astype(BF16)

    def gated(out_off, off, act, j):
        act_ref[0, :, out_off + j:out_off + j + COL_BLOCK] = act(
            _dot(h, w_ref[:, off + j:off + j + COL_BLOCK])).astype(BF16)

    def beta_g():
        bg_ref[0] = _beta_g_rows(_dot(h, wba_ref[...]), alr_ref[...], dtr_ref[...])
        rawt = _dot_nt(wbat_ref[...], h)
        srow = lax.broadcasted_iota(jnp.int32, rawt.shape, 0)
        gt = -jnp.exp(alc_ref[...]) * _softplus(rawt + dtc_ref[...])
        bgt_ref[0] = jnp.where(srow < N_HEADS, _sigmoid(rawt), gt)

    side = []
    for j in range(0, D_QKV, COL_BLOCK):
        side.append(functools.partial(qkv_cols, j))
        side += [functools.partial(short_conv, cb)
                 for cb in range(j // LANES, (j + COL_BLOCK) // LANES)]
    for j in range(0, D_MODEL, COL_BLOCK):
        side.append(functools.partial(gated, ACT_GZ, OFF_Z, _silu, j))
        side.append(functools.partial(gated, ACT_SGD, OFF_GD, _sigmoid, j))
    side.append(beta_g)

    conv = [functools.partial(_strided_conv, ext_ref, yc_ref, wdw_ref, lb, CONV_HALO - (K_CONV - 1),
                              K_CONV, (rb, rb + row_block))
            for rb in range(0, tm, row_block) for lb in range(N_SLABS)]
    _interleave(conv, side)
    for rb in range(0, tm, LN_ROWS):
        y = jnp.concatenate([yc_ref[lb, rb:rb + LN_ROWS, :] for lb in range(N_SLABS)], axis=1)
        y = y + bdw_ref[...]
        c_ref[rb:rb + LN_ROWS, :] = _layer_norm_silu(y, lng_ref[...], lnb_ref[...]).astype(BF16)

    for j in range(0, D_MODEL, COL_BLOCK):
        cs = slice(j, j + COL_BLOCK)
        conv_out = _dot(c_ref[...], wcout_ref[:, cs])
        gate_c = _dot(h, w_ref[:, OFF_GC + j:OFF_GC + j + COL_BLOCK])
        act_ref[0, :, ACT_A + j:ACT_A + j + COL_BLOCK] = (_sigmoid(gate_c) * conv_out).astype(BF16)

    @pl.when(t == nt - 1)
    def _():
        for lb in range(N_SLABS):
            ncv_ref[0, :, lanes(lb)] = ext_ref[lb, tm:tm + CONV_HALO, :]
        for cb in range(N_QKV_SLABS):
            nqk_ref[0, :, lanes(cb)] = qext_ref[cb, tm:tm + SHORT_HALO, :]


def _mixer_in_seq(x, cbuf, qbuf, p, *, tm, row_block):
    B, T, _ = x.shape
    nt = T // tm
    bs = cbuf.shape[0]
    state_idx = (lambda b, t: (b, 0, 0)) if bs == B else (lambda b, t: (0, 0, 0))
    tile = lambda n: pl.BlockSpec((1, tm, n), lambda b, t: (b, t, 0))
    in_specs = [
        tile(D_MODEL),
        pl.BlockSpec((1, CONV_HALO, D_MODEL), state_idx),
        pl.BlockSpec((1, SHORT_HALO, D_QKV), state_idx),
        _resident((1, D_MODEL)),
        _resident(p["wmain"].shape),
        _resident(p["wba"].shape),
        _resident(p["wbat"].shape),
        _resident(p["wdw8"].shape),
        _resident((1, D_MODEL)),
        _resident((1, D_MODEL)),
        _resident((1, D_MODEL)),
        _resident(p["wcout"].shape),
        _resident(p["wsh8"].shape),
        _resident((1, LANES)),
        _resident((1, LANES)),
        _resident((2 * N_HEADS, 1)),
        _resident((2 * N_HEADS, 1)),
    ]
    out_shape = [
        jax.ShapeDtypeStruct((B, T, ACT_WIDTH), BF16),
        jax.ShapeDtypeStruct((B, T, LANES), F32),
        jax.ShapeDtypeStruct((B, 2 * N_HEADS, T), F32),
        jax.ShapeDtypeStruct((B, CONV_HALO, D_MODEL), F32),
        jax.ShapeDtypeStruct((B, SHORT_HALO, D_QKV), F32),
    ]
    out_specs = [
        tile(ACT_WIDTH), tile(LANES),
        pl.BlockSpec((1, 2 * N_HEADS, tm), lambda b, t: (b, 0, t)),
        pl.BlockSpec((1, CONV_HALO, D_MODEL), lambda b, t: (b, 0, 0)),
        pl.BlockSpec((1, SHORT_HALO, D_QKV), lambda b, t: (b, 0, 0)),
    ]
    return pl.pallas_call(
        functools.partial(_mixer_in_seq_kernel, tm=tm, row_block=row_block),
        grid=(B, nt),
        in_specs=in_specs,
        out_specs=out_specs,
        out_shape=out_shape,
        scratch_shapes=[
            pltpu.VMEM((N_SLABS, tm + CONV_HALO, LANES), F32),
            pltpu.VMEM((N_SLABS, tm, LANES), F32),
            pltpu.VMEM((N_QKV_SLABS, tm + SHORT_HALO, LANES), F32),
            pltpu.VMEM((2, tm, LANES), F32),
            pltpu.VMEM((tm, D_MODEL), BF16),
        ],
        compiler_params=pltpu.CompilerParams(
            dimension_semantics=("arbitrary", "arbitrary"), vmem_limit_bytes=VMEM_LIMIT),
        name="mixer_in_seq",
    )(x, cbuf, qbuf, p["gmix"], p["wmain"], p["wba"], p["wbat"], p["wdw8"], p["bdw"], p["lng"],
      p["lnb"], p["wcout"], p["wsh8"], p["alog_row"], p["dt_row"], p["alog_col"], p["dt_col"])


def _mixer_in_dec_kernel(x_ref, cbuf_ref, qbuf_ref, gmix_ref, w_ref, wba_ref, wdw_ref, bdw_ref,
                         lng_ref, lnb_ref, wcout_ref, wsh_ref, alr_ref, dtr_ref,
                         a_ref, qkv_ref, gz_ref, sgd_ref, bg_ref, glu_ref, raw_ref, c_ref, *, ts, nb):
    n = ts * nb
    x = x_ref[...].reshape(n, D_MODEL)
    h = _rms(x, gmix_ref[...]).astype(BF16)

    for j in range(0, D_MODEL, COL_BLOCK):
        a = _dot(h, w_ref[:, OFF_GLU_A + j:OFF_GLU_A + j + COL_BLOCK])
        b = _dot(h, w_ref[:, OFF_GLU_B + j:OFF_GLU_B + j + COL_BLOCK])
        glu_ref[:, :, j:j + COL_BLOCK] = (a * _sigmoid(b)).reshape(ts, nb, COL_BLOCK)

    nbuf = K_CONV - 1
    for t in range(ts):
        acc = jnp.zeros((nb, D_MODEL), F32) + bdw_ref[...]
        for i in range(K_CONV):
            j = t + i
            row = cbuf_ref[j] if j < nbuf else glu_ref[j - nbuf]
            acc = acc + wdw_ref[i:i + 1, :] * row
        c_ref[t * nb:(t + 1) * nb, :] = _layer_norm_silu(acc, lng_ref[...], lnb_ref[...]).astype(BF16)

    conv_out = _dot(c_ref[...], wcout_ref[...])
    gate_c = _dot(h, w_ref[:, OFF_GC:OFF_GC + D_MODEL])
    a_ref[...] = (_sigmoid(gate_c) * conv_out).astype(BF16).reshape(ts, nb, D_MODEL)

    for j in range(0, D_QKV, COL_BLOCK):
        raw_ref[:, :, j:j + COL_BLOCK] = _dot(
            h, w_ref[:, OFF_QKV + j:OFF_QKV + j + COL_BLOCK]).reshape(ts, nb, COL_BLOCK)
    nsh = K_SHORT - 1
    for t in range(ts):
        for cb in range(D_QKV // LANES):
            sl = slice(cb * LANES, (cb + 1) * LANES)
            acc = None
            for i in range(K_SHORT):
                j = t + i
                row = qbuf_ref[j, :, sl] if j < nsh else raw_ref[j - nsh, :, sl]
                term = wsh_ref[i:i + 1, sl] * row
                acc = term if acc is None else acc + term
            qkv_ref[t, :, sl] = _qkv_post(acc, cb).astype(BF16)

    gz_ref[...] = _silu(_dot(h, w_ref[:, OFF_Z:OFF_Z + D_MODEL])).astype(BF16).reshape(ts, nb, D_MODEL)
    sgd_ref[...] = _sigmoid(_dot(h, w_ref[:, OFF_GD:OFF_GD + D_MODEL])).astype(BF16).reshape(
        ts, nb, D_MODEL)
    bg_ref[...] = _beta_g_rows(_dot(h, wba_ref[...]), alr_ref[...], dtr_ref[...]).reshape(ts, nb, LANES)


def _mixer_in_dec(xt, cbuft, qbuft, p, *, nb):
    ts, B, _ = xt.shape
    blk = lambda r, n: pl.BlockSpec((r, nb, n), lambda i: (0, i, 0))
    in_specs = [
        blk(ts, D_MODEL), blk(K_CONV - 1, D_MODEL), blk(K_SHORT - 1, D_QKV),
        _resident((1, D_MODEL)),
        _resident(p["wmain"].shape),
        _resident(p["wba"].shape),
        _resident(p["wdw"].shape),
        _resident((1, D_MODEL)),
        _resident((1, D_MODEL)),
        _resident((1, D_MODEL)),
        _resident(p["wcout"].shape),
        _resident(p["wsh"].shape),
        _resident((1, LANES)),
        _resident((1, LANES)),
    ]
    out_shape = [
        jax.ShapeDtypeStruct((ts, B, D_MODEL), BF16),
        jax.ShapeDtypeStruct((ts, B, D_QKV), BF16),
        jax.ShapeDtypeStruct((ts, B, D_MODEL), BF16),
        jax.ShapeDtypeStruct((ts, B, D_MODEL), BF16),
        jax.ShapeDtypeStruct((ts, B, LANES), F32),
        jax.ShapeDtypeStruct((ts, B, D_MODEL), F32),
        jax.ShapeDtypeStruct((ts, B, D_QKV), F32),
    ]
    out_specs = [blk(ts, D_MODEL), blk(ts, D_QKV), blk(ts, D_MODEL), blk(ts, D_MODEL),
                 blk(ts, LANES), blk(ts, D_MODEL), blk(ts, D_QKV)]
    return pl.pallas_call(
        functools.partial(_mixer_in_dec_kernel, ts=ts, nb=nb),
        grid=(B // nb,),
        in_specs=in_specs,
        out_specs=out_specs,
        out_shape=out_shape,
        scratch_shapes=[pltpu.VMEM((ts * nb, D_MODEL), BF16)],
        compiler_params=pltpu.CompilerParams(
            dimension_semantics=("arbitrary",), vmem_limit_bytes=VMEM_LIMIT),
        name="mixer_in_dec",
    )(xt, cbuft, qbuft, p["gmix"], p["wmain"], p["wba"], p["wdw"], p["bdw"], p["lng"], p["lnb"],
      p["wcout"], p["wsh"], p["alog_row"], p["dt_row"])


def _qkv_head(hd, part):
    return slice(part * D_MODEL + hd * D_HEAD, part * D_MODEL + (hd + 1) * D_HEAD)


def _intra_chunks(qkv_ref, bg_ref, bgt_ref, *, L, cps, ns):
    row = lax.broadcasted_iota(jnp.int32, (L, L), 0)
    col = lax.broadcasted_iota(jnp.int32, (L, L), 1)
    causal = row >= col
    strict = row > col
    tril = jnp.where(causal, 1.0, 0.0).astype(BF16)
    triu = jnp.where(row <= col, 1.0, 0.0).astype(BF16)
    eye = jnp.where(row == col, 1.0, 0.0).astype(F32)

    blocks = [(si, ci) for si in range(ns) for ci in range(cps)]
    items = [(si, ci, hd) for (si, ci) in blocks for hd in range(N_HEADS)]
    each = lambda f: {it: f(*it) for it in items}
    rows = lambda ci: slice(ci * L, (ci + 1) * L)

    bgc, gcum_c, gcum_r = {}, {}, {}
    for si, ci in blocks:
        bgc[si, ci] = bg_ref[si, rows(ci), :]
        c1, c2, c3 = _split3(bgc[si, ci])
        gcum_c[si, ci] = _dot(tril, c1) + _dot(tril, c2) + _dot(tril, c3)
        r1, r2, r3 = _split3(bgt_ref[si, :, rows(ci)])
        gcum_r[si, ci] = _dot(r1, triu) + _dot(r2, triu) + _dot(r3, triu)

    q = each(lambda si, ci, hd: qkv_ref[si, rows(ci), _qkv_head(hd, 0)])
    k = each(lambda si, ci, hd: qkv_ref[si, rows(ci), _qkv_head(hd, 1)])
    v = each(lambda si, ci, hd: qkv_ref[si, rows(ci), _qkv_head(hd, 2)])
    beta = each(lambda si, ci, hd: bgc[si, ci][:, hd:hd + 1])
    gcol = each(lambda si, ci, hd: gcum_c[si, ci][:, N_HEADS + hd:N_HEADS + hd + 1])
    grow = each(lambda si, ci, hd: gcum_r[si, ci][N_HEADS + hd:N_HEADS + hd + 1, :])
    glast = each(lambda si, ci, hd: gcum_c[si, ci][L - 1:L, N_HEADS + hd:N_HEADS + hd + 1])

    decay = each(lambda *it: jnp.where(
        causal, jnp.exp(jnp.where(causal, gcol[it] - grow[it], 0.0)), 0.0))
    kk = each(lambda *it: _dot_nt(k[it], k[it]))
    lm = each(lambda *it: jnp.where(strict, beta[it] * kk[it] * decay[it], 0.0))

    x = each(lambda *it: eye - lm[it])
    pw = lm
    n = 2
    while n < L:
        pb = each(lambda *it: pw[it].astype(BF16))
        pw = each(lambda *it: _dot(pb[it], pb[it]))
        x = each(lambda *it: x[it] + _dot(x[it].astype(BF16), pw[it].astype(BF16)))
        n *= 2

    eg = each(lambda *it: jnp.exp(gcol[it]))
    rhs = each(lambda *it: jnp.concatenate(
        [v[it].astype(F32) * beta[it], k[it].astype(F32) * (beta[it] * eg[it])], axis=1).astype(BF16))
    sol = each(lambda *it: _dot(x[it].astype(BF16), rhs[it]))
    attn = each(lambda *it: (_dot_nt(q[it], k[it]) * decay[it]).astype(BF16))
    q_dec = each(lambda *it: (q[it].astype(F32) * eg[it]).astype(BF16))
    k_dec = each(lambda *it: (k[it].astype(F32) * jnp.exp(glast[it] - gcol[it])).astype(BF16))
    last = each(lambda *it: jnp.exp(glast[it]))
    return sol, attn, (lambda it, u: u), q_dec, k_dec, last


def _intra_chunk_pairs(qkv_ref, bg_ref, bgt_ref, *, L, cps, ns):
    P = 2 * L
    rowp = lax.broadcasted_iota(jnp.int32, (L, P), 0)
    lanep = lax.broadcasted_iota(jnp.int32, (L, P), 1)
    left = lanep < L
    colp = jnp.where(left, lanep, lanep - L)
    causal = rowp >= colp
    strict = rowp > colp
    eye = jnp.where(rowp == colp, 1.0, 0.0).astype(F32)
    r2 = lax.broadcasted_iota(jnp.int32, (P, P), 0)
    c2 = lax.broadcasted_iota(jnp.int32, (P, P), 1)
    same = (r2 < L) == (c2 < L)
    tril = jnp.where(jnp.logical_and(same, r2 >= c2), 1.0, 0.0).astype(BF16)
    triu = jnp.where(jnp.logical_and(same, r2 <= c2), 1.0, 0.0).astype(BF16)

    pairs = [(si, m) for si in range(ns) for m in range(cps // 2)]
    items = [(si, m, hd) for (si, m) in pairs for hd in range(N_HEADS)]
    each = lambda f: {it: f(*it) for it in items}
    rows = lambda m: slice(m * P, (m + 1) * P)
    side_by_side = lambda a: jnp.where(left, a[0:L], a[L:P])
    block_diag = lambda a: jnp.concatenate(
        [jnp.where(left, a, 0.0), jnp.where(left, 0.0, a)], axis=0).astype(BF16)

    bgc, gcum_c, gcum_r = {}, {}, {}
    for si, m in pairs:
        bgc[si, m] = bg_ref[si, rows(m), :]
        c1, c2_, c3 = _split3(bgc[si, m])
        gcum_c[si, m] = _dot(tril, c1) + _dot(tril, c2_) + _dot(tril, c3)
        r1, r2_, r3 = _split3(bgt_ref[si, :, rows(m)])
        gcum_r[si, m] = _dot(r1, triu) + _dot(r2_, triu) + _dot(r3, triu)

    q = each(lambda si, m, hd: qkv_ref[si, rows(m), _qkv_head(hd, 0)])
    k = each(lambda si, m, hd: qkv_ref[si, rows(m), _qkv_head(hd, 1)])
    v = each(lambda si, m, hd: qkv_ref[si, rows(m), _qkv_head(hd, 2)])
    beta = each(lambda si, m, hd: bgc[si, m][:, hd:hd + 1])
    gcol = each(lambda si, m, hd: gcum_c[si, m][:, N_HEADS + hd:N_HEADS + hd + 1])
    grow = each(lambda si, m, hd: gcum_r[si, m][N_HEADS + hd:N_HEADS + hd + 1, :])

    decay = each(lambda *it: jnp.where(
        causal, jnp.exp(jnp.where(causal, side_by_side(gcol[it]) - grow[it], 0.0)), 0.0))
    kq = each(lambda *it: _dot_nt(jnp.concatenate([k[it], q[it]], axis=0), k[it]))
    kk = each(lambda *it: side_by_side(kq[it][0:P]))
    lm = each(lambda *it: jnp.where(strict, side_by_side(beta[it]) * kk[it] * decay[it], 0.0))

    x = each(lambda *it: eye - lm[it])
    pw = each(lambda *it: _dot(lm[it].astype(BF16), block_diag(lm[it])))
    n = 2
    while n < L:
        w = each(lambda *it: block_diag(pw[it]))
        if 2 * n < L:
            r = each(lambda *it: _dot(jnp.concatenate([x[it], pw[it]], axis=0).astype(BF16), w[it]))
            x = each(lambda *it: x[it] + r[it][0:L])
            pw = each(lambda *it: r[it][L:P])
        else:
            x = each(lambda *it: x[it] + _dot(x[it].astype(BF16), w[it]))
        n *= 2

    eg = each(lambda *it: jnp.exp(gcol[it]))
    rhs = each(lambda *it: jnp.concatenate(
        [v[it].astype(F32) * beta[it], k[it].astype(F32) * (beta[it] * eg[it])], axis=1).astype(BF16))
    sol2 = each(lambda *it: _dot(block_diag(x[it]), rhs[it]))
    attn2 = each(lambda *it: side_by_side(kq[it][P:2 * P]) * decay[it])
    q_dec2 = each(lambda *it: (q[it].astype(F32) * eg[it]).astype(BF16))

    sol, attn, q_dec, k_dec, last = {}, {}, {}, {}, {}
    for it in items:
        si, m, hd = it
        for half in range(2):
            key = (si, 2 * m + half, hd)
            hr = slice(half * L, (half + 1) * L)
            glast = gcum_c[si, m][(half + 1) * L - 1:(half + 1) * L, N_HEADS + hd:N_HEADS + hd + 1]
            sol[key] = sol2[it][hr]
            attn[key] = jnp.where(left if half == 0 else jnp.logical_not(left), attn2[it], 0.0).astype(BF16)
            q_dec[key] = q_dec2[it][hr]
            k_dec[key] = (k[it][hr].astype(F32) * jnp.exp(glast - gcol[it][hr])).astype(BF16)
            last[key] = jnp.exp(glast)

    def attn_rhs(key, u):
        z = jnp.zeros_like(u)
        return jnp.concatenate([u, z] if key[1] % 2 == 0 else [z, u], axis=0)

    return sol, attn, attn_rhs, q_dec, k_dec, last


def _delta_kernel(qkv_ref, gz_ref, bg_ref, bgt_ref, s0_ref, ghead_ref, og_ref, sout_ref, s_ref,
                  *, L, cps, ns):
    c = pl.program_id(1)
    nc = pl.num_programs(1)

    @pl.when(c == 0)
    def _():
        for si in range(ns):
            s_ref[si] = s0_ref[min(si, s0_ref.shape[0] - 1)]

    ghead = ghead_ref[...]
    rows = lambda ci: slice(ci * L, (ci + 1) * L)
    head = lambda hd: slice(hd * D_HEAD, (hd + 1) * D_HEAD)
    intra = _intra_chunk_pairs if (cps % 2 == 0 and 2 * L == LANES) else _intra_chunks
    sol, attn, attn_rhs, q_dec, k_dec, last = intra(qkv_ref, bg_ref, bgt_ref, L=L, cps=cps, ns=ns)

    state = {(si, hd): s_ref[si, hd] for si in range(ns) for hd in range(N_HEADS)}
    for ci in range(cps):
        cur = [(si, ci, hd) for si in range(ns) for hd in range(N_HEADS)]
        now = lambda f: {it: f(*it) for it in cur}
        sb = now(lambda si, ci, hd: state[si, hd].astype(BF16))
        ks = now(lambda *it: _dot(jnp.concatenate(
            [sol[it][:, D_HEAD:].astype(BF16), q_dec[it]], axis=0), sb[it]))
        ub = now(lambda *it: (sol[it][:, :D_HEAD] - ks[it][0:L]).astype(BF16))
        o = now(lambda *it: ks[it][L:2 * L] + _dot(attn[it], attn_rhs(it, ub[it])))
        for it in cur:
            si, _, hd = it
            state[si, hd] = state[si, hd] * last[it] + _dot_tn(k_dec[it], ub[it])
            on = _rms(o[it], ghead) * gz_ref[si, rows(ci), head(hd)].astype(F32)
            og_ref[si, rows(ci), head(hd)] = on.astype(BF16)
    for (si, hd), s in state.items():
        s_ref[si, hd] = s

    @pl.when(c == nc - 1)
    def _():
        sout_ref[...] = s_ref[...]


def _delta(qkv, gz, bg, bgt, s0, ghead, *, L, cps, ns, gz_col=0):
    B, T, _ = qkv.shape
    tb = L * cps
    bs = s0.shape[0]
    assert bs == B or (bs == 1 and ns == 1)
    s_spec = (pl.BlockSpec((ns, N_HEADS, D_HEAD, D_HEAD), lambda b, c: (b, 0, 0, 0)) if bs == B else
              pl.BlockSpec((1, N_HEADS, D_HEAD, D_HEAD), lambda b, c: (0, 0, 0, 0)))
    tile = lambda n, col=0: pl.BlockSpec((ns, tb, n), lambda b, c: (b, c, col))
    return pl.pallas_call(
        functools.partial(_delta_kernel, L=L, cps=cps, ns=ns),
        grid=(B // ns, T // tb),
        in_specs=[
            tile(D_QKV), tile(D_MODEL, gz_col), tile(LANES),
            pl.BlockSpec((ns, 2 * N_HEADS, tb), lambda b, c: (b, 0, c)),
            s_spec,
            _resident((1, D_HEAD)),
        ],
        out_specs=[
            tile(D_MODEL),
            pl.BlockSpec((ns, N_HEADS, D_HEAD, D_HEAD), lambda b, c: (b, 0, 0, 0)),
        ],
        out_shape=[
            jax.ShapeDtypeStruct((B, T, D_MODEL), BF16),
            jax.ShapeDtypeStruct((B, N_HEADS, D_HEAD, D_HEAD), F32),
        ],
        scratch_shapes=[pltpu.VMEM((ns, N_HEADS, D_HEAD, D_HEAD), F32)],
        compiler_params=pltpu.CompilerParams(
            dimension_semantics=("arbitrary", "arbitrary"), vmem_limit_bytes=VMEM_LIMIT),
        name="delta_rule",
    )(qkv, gz, bg, bgt, s0, ghead)


def _mixer_out_kernel(x_ref, a_ref, og_ref, sgd_ref, wdout_ref, wo_ref, gmlp_ref, wup_ref, wdown_ref,
                      gfin_ref, y_ref):
    dn = _dot(og_ref[...], wdout_ref[...])
    mix = a_ref[...].astype(F32) + sgd_ref[...].astype(F32) * dn
    x2 = x_ref[...] + _dot(mix.astype(BF16), wo_ref[...])
    hn = _rms(x2, gmlp_ref[...]).astype(BF16)
    acc = x2
    for j in range(0, D_FF, D_MODEL):
        hm = jnp.maximum(_dot(hn, wup_ref[:, j:j + D_MODEL]), 0.0)
        acc = acc + _dot((hm * hm).astype(BF16), wdown_ref[j:j + D_MODEL, :])
    y_ref[...] = _rms(acc, gfin_ref[...])


def _mixer_out(x, a, og, sgd, p, *, tm, a_col=0, sgd_col=0):
    n = x.shape[0]
    tile = pl.BlockSpec((tm, D_MODEL), lambda i: (i, 0))
    col_tile = lambda col: pl.BlockSpec((tm, D_MODEL), lambda i: (i, col))
    return pl.pallas_call(
        _mixer_out_kernel,
        grid=(n // tm,),
        in_specs=[tile, col_tile(a_col), tile, col_tile(sgd_col),
                  _resident(p["wdout"].shape), _resident(p["wo"].shape), _resident((1, D_MODEL)),
                  _resident(p["wup"].shape), _resident(p["wdown"].shape), _resident((1, D_MODEL))],
        out_specs=tile,
        out_shape=jax.ShapeDtypeStruct((n, D_MODEL), F32),
        compiler_params=pltpu.CompilerParams(
            dimension_semantics=("arbitrary",), vmem_limit_bytes=VMEM_LIMIT),
        name="mixer_out",
    )(x, a, og, sgd, p["wdout"], p["wo"], p["gmlp"], p["wup"], p["wdown"], p["gfin"])


def _prep_params(g_mix, w_in, w_dw, b_dw, ln_g, ln_b, w_cout, w_short, a_log, dt_bias, g_head,
                 w_dout, w_o, g_mlp, w_up, w_down, g_final):
    w = w_in[0]
    n_main = 2 * D_MODEL + D_QKV + D_MODEL
    ba = w[:, n_main:n_main + 2 * N_HEADS]
    pad16 = lambda v: jnp.pad(v.astype(F32), (N_HEADS, 0))
    rows8 = lambda v: jnp.broadcast_to(v[:, None, :], (v.shape[0], SUBLANES, v.shape[1]))
    return dict(
        wmain=jnp.concatenate([w[:, :n_main], w[:, n_main + 2 * N_HEADS:]], axis=1).astype(BF16),
        wba=jnp.pad(ba, ((0, 0), (0, LANES - 2 * N_HEADS))).astype(BF16),
        wbat=ba.T.astype(BF16),
        gmix=g_mix[0].reshape(1, D_MODEL),
        wdw=w_dw[0], wdw8=rows8(w_dw[0]), bdw=b_dw[0].reshape(1, D_MODEL),
        lng=ln_g[0].reshape(1, D_MODEL), lnb=ln_b[0].reshape(1, D_MODEL),
        wcout=w_cout[0].astype(BF16), wsh=w_short[0], wsh8=rows8(w_short[0]),
        alog_row=jnp.pad(pad16(a_log[0]), (0, LANES - 2 * N_HEADS)).reshape(1, LANES),
        dt_row=jnp.pad(pad16(dt_bias[0]), (0, LANES - 2 * N_HEADS)).reshape(1, LANES),
        alog_col=pad16(a_log[0]).reshape(2 * N_HEADS, 1),
        dt_col=pad16(dt_bias[0]).reshape(2 * N_HEADS, 1),
        ghead=g_head[0].reshape(1, D_HEAD),
        wdout=w_dout[0].astype(BF16), wo=w_o[0].astype(BF16), gmlp=g_mlp[0].reshape(1, D_MODEL),
        wup=w_up[0].astype(BF16), wdown=w_down[0].astype(BF16), gfin=g_final.reshape(1, D_MODEL),
    )


def kernel(x_prompt, x_sample, state_conv, state_qkv_conv, state_delta, meta_tokens, g_mix, w_in, w_dw, b_dw, ln_g, ln_b, w_cout, w_short, a_log, dt_bias, g_head, w_dout, w_o, g_mlp, w_up, w_down, g_final):
    p = _prep_params(g_mix, w_in, w_dw, b_dw, ln_g, ln_b, w_cout, w_short, a_log, dt_bias, g_head,
                     w_dout, w_o, g_mlp, w_up, w_down, g_final)
    bp, seq, _ = x_prompt.shape
    bd, ts, _ = x_sample.shape

    zc = jnp.zeros((1, CONV_HALO, D_MODEL), F32)
    zq = jnp.zeros((1, SHORT_HALO, D_QKV), F32)
    zs = jnp.zeros((1, N_HEADS, D_HEAD, D_HEAD), F32)
    xm = meta_tokens.astype(F32)[None]
    gz_col, sgd_col, a_col = ACT_GZ // D_MODEL, ACT_SGD // D_MODEL, ACT_A // D_MODEL
    act_m, bg_m, bgt_m, cv_m, qk_m = _mixer_in_seq(xm, zc, zq, p, tm=N_META, row_block=N_META)
    _, s_m = _delta(act_m, act_m, bg_m, bgt_m, zs, p["ghead"], L=N_META, cps=1, ns=1, gz_col=gz_col)

    act_p, bg_p, bgt_p, cv_p, qk_p = _mixer_in_seq(
        x_prompt, cv_m, qk_m, p, tm=TM_IN_PROMPT, row_block=CONV_ROW_BLOCK)
    og_p, s_p = _delta(act_p, act_p, bg_p, bgt_p, s_m, p["ghead"], L=DELTA_CHUNK,
                       cps=DELTA_CHUNKS_PER_STEP, ns=1, gz_col=gz_col)
    n_p = bp * seq
    act_flat = act_p.reshape(n_p, ACT_WIDTH)
    y_prompt = _mixer_out(x_prompt.reshape(n_p, D_MODEL), act_flat, og_p.reshape(n_p, D_MODEL), act_flat,
                          p, tm=TM_OUT_PROMPT, a_col=a_col, sgd_col=sgd_col).reshape(bp, seq, D_MODEL)

    tmaj = lambda v: jnp.transpose(v, (1, 0, 2))
    a_s, qkv_s, gz_s, sgd_s, bg_s, glu_s, raw_s = _mixer_in_dec(
        tmaj(x_sample), tmaj(state_conv[0]), tmaj(state_qkv_conv[0]), p, nb=DEC_BATCH_BLOCK)
    padt = lambda v: jnp.pad(tmaj(v), ((0, 0), (0, DEC_PAD - ts), (0, 0)))
    bgt_s = jnp.pad(jnp.transpose(bg_s[:, :, :2 * N_HEADS], (1, 2, 0)), ((0, 0), (0, 0), (0, DEC_PAD - ts)))
    og_s, s_s = _delta(padt(qkv_s), padt(gz_s), padt(bg_s), bgt_s, state_delta[0], p["ghead"],
                       L=DEC_PAD, cps=1, ns=DEC_SEQS_PER_STEP)
    n_s = bd * ts
    y_sample = _mixer_out(x_sample.reshape(n_s, D_MODEL), tmaj(a_s).reshape(n_s, D_MODEL),
                          og_s[:, :ts].reshape(n_s, D_MODEL), tmaj(sgd_s).reshape(n_s, D_MODEL), p,
                          tm=TM_OUT_DECODE).reshape(bd, ts, D_MODEL)

    nbuf, nsh = K_CONV - 1, K_SHORT - 1
    prompt_conv = cv_p[:, CONV_HALO - nbuf:][None]
    prompt_qkv_conv = qk_p[:, SHORT_HALO - nsh:][None]
    sample_conv = jnp.concatenate([state_conv[0][:, ts:], tmaj(glu_s)], axis=1)[None]
    sample_qkv_conv = tmaj(raw_s[ts - nsh:])[None]
    return (y_prompt, y_sample, prompt_conv, prompt_qkv_conv, s_p[None],
            sample_conv, sample_qkv_conv, s_s[None])
```

```python
import functools

import jax
import jax.numpy as jnp
from jax import lax
from jax.experimental import pallas as pl
from jax.experimental.pallas import tpu as pltpu

F32 = jnp.float32
BF16 = jnp.bfloat16

D_MODEL = 1024
N_HEADS = 8
D_HEAD = 128
D_QKV = 3 * D_MODEL
D_FF = 4 * D_MODEL
K_CONV = 31
K_SHORT = 4
N_META = 16
EPS = 1e-6

CONV_HALO = 32
SHORT_HALO = 8
LANES = 128
SUBLANES = 8
COL_BLOCK = 512
LN_ROWS = 16
ROW_STRIDE = 2
N_SLABS = D_MODEL // LANES
N_QKV_SLABS = D_QKV // LANES

OFF_GLU_A, OFF_GLU_B, OFF_QKV, OFF_Z, OFF_GC, OFF_GD = 0, 1024, 2048, 5120, 6144, 7168
ACT_GZ, ACT_SGD, ACT_A, ACT_WIDTH = D_QKV, D_QKV + D_MODEL, D_QKV + 2 * D_MODEL, D_QKV + 3 * D_MODEL

VMEM_LIMIT = 60 * 1024 * 1024

TM_IN_PROMPT = 512
TM_OUT_PROMPT = 512
TM_OUT_DECODE = 256
CONV_ROW_BLOCK = 64
DELTA_CHUNK = 64
DELTA_CHUNKS_PER_STEP = 16
DEC_PAD = 16
DEC_SEQS_PER_STEP = 8
DEC_BATCH_BLOCK = 64


def _dot(a, b):
    return jnp.dot(a, b, preferred_element_type=F32)


def _dot_nt(a, b):
    return lax.dot_general(a, b, (((1,), (1,)), ((), ())), preferred_element_type=F32)


def _dot_tn(a, b):
    return lax.dot_general(a, b, (((0,), (0,)), ((), ())), preferred_element_type=F32)


def _sigmoid(x):
    return 0.5 * jnp.tanh(0.5 * x) + 0.5


def _silu(x):
    hx = 0.5 * x
    return hx * jnp.tanh(hx) + hx


def _softplus(x):
    return jnp.maximum(x, 0.0) + jnp.log(1.0 + jnp.exp(-jnp.abs(x)))


def _rms(x, g):
    return x * lax.rsqrt(jnp.mean(x * x, axis=-1, keepdims=True) + EPS) * g


def _layer_norm_silu(c, g, b):
    mu = jnp.mean(c, axis=-1, keepdims=True)
    d = c - mu
    var = jnp.mean(d * d, axis=-1, keepdims=True)
    return _silu(d * lax.rsqrt(var + EPS) * g + b)


def _split3(x):
    x1 = x.astype(BF16)
    r1 = x - x1.astype(F32)
    x2 = r1.astype(BF16)
    x3 = (r1 - x2.astype(F32)).astype(BF16)
    return x1, x2, x3


def _resident(shape):
    nd = len(shape)
    return pl.BlockSpec(shape, lambda *_: (0,) * nd, pipeline_mode=pl.Buffered(1))


def _beta_g_rows(raw, alog_row, dt_row):
    lane = lax.broadcasted_iota(jnp.int32, raw.shape, 1)
    g = -jnp.exp(alog_row) * _softplus(raw + dt_row)
    return jnp.where(lane < N_HEADS, _sigmoid(raw), g)


def _qkv_post(acc, cb):
    y = _silu(acc)
    if cb < 2 * N_HEADS:
        y = y * lax.rsqrt(jnp.sum(y * y, axis=-1, keepdims=True) + 1e-6)
        if cb < N_HEADS:
            y = y * (D_HEAD ** -0.5)
    return y


def _interleave(main, side):
    done = 0
    for idx, task in enumerate(main):
        task()
        while done * len(main) < (idx + 1) * len(side):
            side[done]()
            done += 1
    for task in side[done:]:
        task()


def _row_groups(lo, hi):
    span = ROW_STRIDE * SUBLANES
    return [lo + m + ph for m in range(0, hi - lo, span) for ph in range(ROW_STRIDE)]


def _strided_conv(src_ref, dst_ref, w_ref, slab, first, n_taps, rows, dst_slab=None):
    ls = slice(slab * LANES, (slab + 1) * LANES)
    starts = _row_groups(*rows)
    accs = [None] * len(starts)
    for i in range(n_taps):
        w = w_ref[i, :, ls]
        for gi, s in enumerate(starts):
            term = w * src_ref[slab, pl.ds(s + first + i, SUBLANES, stride=ROW_STRIDE), :]
            accs[gi] = term if accs[gi] is None else accs[gi] + term
    dst_slab = slab if dst_slab is None else dst_slab
    for gi, s in enumerate(starts):
        dst_ref[dst_slab, pl.ds(s, SUBLANES, stride=ROW_STRIDE), :] = accs[gi]


def _mixer_in_seq_kernel(x_ref, cbuf_ref, qbuf_ref, gmix_ref, w_ref, wba_ref, wbat_ref, wdw_ref,
                         bdw_ref, lng_ref, lnb_ref, wcout_ref, wsh_ref, alr_ref, dtr_ref, alc_ref,
                         dtc_ref, act_ref, bg_ref, bgt_ref, ncv_ref, nqk_ref,
                         ext_ref, yc_ref, qext_ref, qy_ref, c_ref, *, tm, row_block):
    t = pl.program_id(1)
    nt = pl.num_programs(1)
    lanes = lambda k: slice(k * LANES, (k + 1) * LANES)

    @pl.when(t == 0)
    def _():
        for lb in range(N_SLABS):
            ext_ref[lb, 0:CONV_HALO, :] = cbuf_ref[0, :, lanes(lb)]
        for cb in range(N_QKV_SLABS):
            qext_ref[cb, 0:SHORT_HALO, :] = qbuf_ref[0, :, lanes(cb)]

    @pl.when(t > 0)
    def _():
        for lb in range(N_SLABS):
            ext_ref[lb, 0:CONV_HALO, :] = ext_ref[lb, tm:tm + CONV_HALO, :]
        for cb in range(N_QKV_SLABS):
            qext_ref[cb, 0:SHORT_HALO, :] = qext_ref[cb, tm:tm + SHORT_HALO, :]

    x = x_ref[0]
    h = _rms(x, gmix_ref[...]).astype(BF16)

    for j in range(0, D_MODEL, COL_BLOCK):
        a = _dot(h, w_ref[:, OFF_GLU_A + j:OFF_GLU_A + j + COL_BLOCK])
        b = _dot(h, w_ref[:, OFF_GLU_B + j:OFF_GLU_B + j + COL_BLOCK])
        glu = a * _sigmoid(b)
        for c in range(0, COL_BLOCK, LANES):
            ext_ref[(j + c) // LANES, CONV_HALO:CONV_HALO + tm, :] = glu[:, c:c + LANES]

    def qkv_cols(j):
        raw = _dot(h, w_ref[:, OFF_QKV + j:OFF_QKV + j + COL_BLOCK])
        for c in range(0, COL_BLOCK, LANES):
            qext_ref[(j + c) // LANES, SHORT_HALO:SHORT_HALO + tm, :] = raw[:, c:c + LANES]

    def short_conv(cb):
        buf = cb % qy_ref.shape[0]
        _strided_conv(qext_ref, qy_ref, wsh_ref, cb, SHORT_HALO - (K_SHORT - 1), K_SHORT, (0, tm),
                      dst_slab=buf)
        act_ref[0, :, lanes(cb)] = _qkv_post(qy_ref[buf], cb).astype(BF16)

    def gated(out_off, off, act, j):
        act_ref[0, :, out_off + j:out_off + j + COL_BLOCK] = act(
            _dot(h, w_ref[:, off + j:off + j + COL_BLOCK])).astype(BF16)

    def beta_g():
        bg_ref[0] = _beta_g_rows(_dot(h, wba_ref[...]), alr_ref[...], dtr_ref[...])
        rawt = _dot_nt(wbat_ref[...], h)
        srow = lax.broadcasted_iota(jnp.int32, rawt.shape, 0)
        gt = -jnp.exp(alc_ref[...]) * _softplus(rawt + dtc_ref[...])
        bgt_ref[0] = jnp.where(srow < N_HEADS, _sigmoid(rawt), gt)

    side = []
    for j in range(0, D_QKV, COL_BLOCK):
        side.append(functools.partial(qkv_cols, j))
        side += [functools.partial(short_conv, cb)
                 for cb in range(j // LANES, (j + COL_BLOCK) // LANES)]
    for j in range(0, D_MODEL, COL_BLOCK):
        side.append(functools.partial(gated, ACT_GZ, OFF_Z, _silu, j))
        side.append(functools.partial(gated, ACT_SGD, OFF_GD, _sigmoid, j))
    side.append(beta_g)

    conv = [functools.partial(_strided_conv, ext_ref, yc_ref, wdw_ref, lb, CONV_HALO - (K_CONV - 1),
                              K_CONV, (rb, rb + row_block))
            for rb in range(0, tm, row_block) for lb in range(N_SLABS)]
    _interleave(conv, side)
    for rb in range(0, tm, LN_ROWS):
        y = jnp.concatenate([yc_ref[lb, rb:rb + LN_ROWS, :] for lb in range(N_SLABS)], axis=1)
        y = y + bdw_ref[...]
        c_ref[rb:rb + LN_ROWS, :] = _layer_norm_silu(y, lng_ref[...], lnb_ref[...]).astype(BF16)

    for j in range(0, D_MODEL, COL_BLOCK):
        cs = slice(j, j + COL_BLOCK)
        conv_out = _dot(c_ref[...], wcout_ref[:, cs])
        gate_c = _dot(h, w_ref[:, OFF_GC + j:OFF_GC + j + COL_BLOCK])
        act_ref[0, :, ACT_A + j:ACT_A + j + COL_BLOCK] = (_sigmoid(gate_c) * conv_out).astype(BF16)

    @pl.when(t == nt - 1)
    def _():
        for lb in range(N_SLABS):
            ncv_ref[0, :, lanes(lb)] = ext_ref[lb, tm:tm + CONV_HALO, :]
        for cb in range(N_QKV_SLABS):
            nqk_ref[0, :, lanes(cb)] = qext_ref[cb, tm:tm + SHORT_HALO, :]


def _mixer_in_seq(x, cbuf, qbuf, p, *, tm, row_block):
    B, T, _ = x.shape
    nt = T // tm
    bs = cbuf.shape[0]
    state_idx = (lambda b, t: (b, 0, 0)) if bs == B else (lambda b, t: (0, 0, 0))
    tile = lambda n: pl.BlockSpec((1, tm, n), lambda b, t: (b, t, 0))
    in_specs = [
        tile(D_MODEL),
        pl.BlockSpec((1, CONV_HALO, D_MODEL), state_idx),
        pl.BlockSpec((1, SHORT_HALO, D_QKV), state_idx),
        _resident((1, D_MODEL)),
        _resident(p["wmain"].shape),
        _resident(p["wba"].shape),
        _resident(p["wbat"].shape),
        _resident(p["wdw8"].shape),
        _resident((1, D_MODEL)),
        _resident((1, D_MODEL)),
        _resident((1, D_MODEL)),
        _resident(p["wcout"].shape),
        _resident(p["wsh8"].shape),
        _resident((1, LANES)),
        _resident((1, LANES)),
        _resident((2 * N_HEADS, 1)),
        _resident((2 * N_HEADS, 1)),
    ]
    out_shape = [
        jax.ShapeDtypeStruct((B, T, ACT_WIDTH), BF16),
        jax.ShapeDtypeStruct((B, T, LANES), F32),
        jax.ShapeDtypeStruct((B, 2 * N_HEADS, T), F32),
        jax.ShapeDtypeStruct((B, CONV_HALO, D_MODEL), F32),
        jax.ShapeDtypeStruct((B, SHORT_HALO, D_QKV), F32),
    ]
    out_specs = [
        tile(ACT_WIDTH), tile(LANES),
        pl.BlockSpec((1, 2 * N_HEADS, tm), lambda b, t: (b, 0, t)),
        pl.BlockSpec((1, CONV_HALO, D_MODEL), lambda b, t: (b, 0, 0)),
        pl.BlockSpec((1, SHORT_HALO, D_QKV), lambda b, t: (b, 0, 0)),
    ]
    return pl.pallas_call(
        functools.partial(_mixer_in_seq_kernel, tm=tm, row_block=row_block),
        grid=(B, nt),
        in_specs=in_specs,
        out_specs=out_specs,
        out_shape=out_shape,
        scratch_shapes=[
            pltpu.VMEM((N_SLABS, tm + CONV_HALO, LANES), F32),
            pltpu.VMEM((N_SLABS, tm, LANES), F32),
            pltpu.VMEM((N_QKV_SLABS, tm + SHORT_HALO, LANES), F32),
            pltpu.VMEM((2, tm, LANES), F32),
            pltpu.VMEM((tm, D_MODEL), BF16),
        ],
        compiler_params=pltpu.CompilerParams(
            dimension_semantics=("arbitrary", "arbitrary"), vmem_limit_bytes=VMEM_LIMIT),
        name="mixer_in_seq",
    )(x, cbuf, qbuf, p["gmix"], p["wmain"], p["wba"], p["wbat"], p["wdw8"], p["bdw"], p["lng"],
      p["lnb"], p["wcout"], p["wsh8"], p["alog_row"], p["dt_row"], p["alog_col"], p["dt_col"])


def _mixer_in_dec_kernel(x_ref, cbuf_ref, qbuf_ref, gmix_ref, w_ref, wba_ref, wdw_ref, bdw_ref,
                         lng_ref, lnb_ref, wcout_ref, wsh_ref, alr_ref, dtr_ref,
                         a_ref, qkv_ref, gz_ref, sgd_ref, bg_ref, glu_ref, raw_ref, c_ref, *, ts, nb):
    n = ts * nb
    x = x_ref[...].reshape(n, D_MODEL)
    h = _rms(x, gmix_ref[...]).astype(BF16)

    for j in range(0, D_MODEL, COL_BLOCK):
        a = _dot(h, w_ref[:, OFF_GLU_A + j:OFF_GLU_A + j + COL_BLOCK])
        b = _dot(h, w_ref[:, OFF_GLU_B + j:OFF_GLU_B + j + COL_BLOCK])
        glu_ref[:, :, j:j + COL_BLOCK] = (a * _sigmoid(b)).reshape(ts, nb, COL_BLOCK)

    nbuf = K_CONV - 1
    for t in range(ts):
        acc = jnp.zeros((nb, D_MODEL), F32) + bdw_ref[...]
        for i in range(K_CONV):
            j = t + i
            row = cbuf_ref[j] if j < nbuf else glu_ref[j - nbuf]
            acc = acc + wdw_ref[i:i + 1, :] * row
        c_ref[t * nb:(t + 1) * nb, :] = _layer_norm_silu(acc, lng_ref[...], lnb_ref[...]).astype(BF16)

    conv_out = _dot(c_ref[...], wcout_ref[...])
    gate_c = _dot(h, w_ref[:, OFF_GC:OFF_GC + D_MODEL])
    a_ref[...] = (_sigmoid(gate_c) * conv_out).astype(BF16).reshape(ts, nb, D_MODEL)

    for j in range(0, D_QKV, COL_BLOCK):
        raw_ref[:, :, j:j + COL_BLOCK] = _dot(
            h, w_ref[:, OFF_QKV + j:OFF_QKV + j + COL_BLOCK]).reshape(ts, nb, COL_BLOCK)
    nsh = K_SHORT - 1
    for t in range(ts):
        for cb in range(D_QKV // LANES):
            sl = slice(cb * LANES, (cb + 1) * LANES)
            acc = None
            for i in range(K_SHORT):
                j = t + i
                row = qbuf_ref[j, :, sl] if j < nsh else raw_ref[j - nsh, :, sl]
                term = wsh_ref[i:i + 1, sl] * row
                acc = term if acc is None else acc + term
            qkv_ref[t, :, sl] = _qkv_post(acc, cb).astype(BF16)

    gz_ref[...] = _silu(_dot(h, w_ref[:, OFF_Z:OFF_Z + D_MODEL])).astype(BF16).reshape(ts, nb, D_MODEL)
    sgd_ref[...] = _sigmoid(_dot(h, w_ref[:, OFF_GD:OFF_GD + D_MODEL])).astype(BF16).reshape(
        ts, nb, D_MODEL)
    bg_ref[...] = _beta_g_rows(_dot(h, wba_ref[...]), alr_ref[...], dtr_ref[...]).reshape(ts, nb, LANES)


def _mixer_in_dec(xt, cbuft, qbuft, p, *, nb):
    ts, B, _ = xt.shape
    blk = lambda r, n: pl.BlockSpec((r, nb, n), lambda i: (0, i, 0))
    in_specs = [
        blk(ts, D_MODEL), blk(K_CONV - 1, D_MODEL), blk(K_SHORT - 1, D_QKV),
        _resident((1, D_MODEL)),
        _resident(p["wmain"].shape),
        _resident(p["wba"].shape),
        _resident(p["wdw"].shape),
        _resident((1, D_MODEL)),
        _resident((1, D_MODEL)),
        _resident((1, D_MODEL)),
        _resident(p["wcout"].shape),
        _resident(p["wsh"].shape),
        _resident((1, LANES)),
        _resident((1, LANES)),
    ]
    out_shape = [
        jax.ShapeDtypeStruct((ts, B, D_MODEL), BF16),
        jax.ShapeDtypeStruct((ts, B, D_QKV), BF16),
        jax.ShapeDtypeStruct((ts, B, D_MODEL), BF16),
        jax.ShapeDtypeStruct((ts, B, D_MODEL), BF16),
        jax.ShapeDtypeStruct((ts, B, LANES), F32),
        jax.ShapeDtypeStruct((ts, B, D_MODEL), F32),
        jax.ShapeDtypeStruct((ts, B, D_QKV), F32),
    ]
    out_specs = [blk(ts, D_MODEL), blk(ts, D_QKV), blk(ts, D_MODEL), blk(ts, D_MODEL),
                 blk(ts, LANES), blk(ts, D_MODEL), blk(ts, D_QKV)]
    return pl.pallas_call(
        functools.partial(_mixer_in_dec_kernel, ts=ts, nb=nb),
        grid=(B // nb,),
        in_specs=in_specs,
        out_specs=out_specs,
        out_shape=out_shape,
        scratch_shapes=[pltpu.VMEM((ts * nb, D_MODEL), BF16)],
        compiler_params=pltpu.CompilerParams(
            dimension_semantics=("arbitrary",), vmem_limit_bytes=VMEM_LIMIT),
        name="mixer_in_dec",
    )(xt, cbuft, qbuft, p["gmix"], p["wmain"], p["wba"], p["wdw"], p["bdw"], p["lng"], p["lnb"],
      p["wcout"], p["wsh"], p["alog_row"], p["dt_row"])


def _qkv_head(hd, part):
    return slice(part * D_MODEL + hd * D_HEAD, part * D_MODEL + (hd + 1) * D_HEAD)


def _intra_chunks(qkv_ref, bg_ref, bgt_ref, *, L, cps, ns):
    row = lax.broadcasted_iota(jnp.int32, (L, L), 0)
    col = lax.broadcasted_iota(jnp.int32, (L, L), 1)
    causal = row >= col
    strict = row > col
    tril = jnp.where(causal, 1.0, 0.0).astype(BF16)
    triu = jnp.where(row <= col, 1.0, 0.0).astype(BF16)
    eye = jnp.where(row == col, 1.0, 0.0).astype(F32)

    blocks = [(si, ci) for si in range(ns) for ci in range(cps)]
    items = [(si, ci, hd) for (si, ci) in blocks for hd in range(N_HEADS)]
    each = lambda f: {it: f(*it) for it in items}
    rows = lambda ci: slice(ci * L, (ci + 1) * L)

    bgc, gcum_c, gcum_r = {}, {}, {}
    for si, ci in blocks:
        bgc[si, ci] = bg_ref[si, rows(ci), :]
        c1, c2, c3 = _split3(bgc[si, ci])
        gcum_c[si, ci] = _dot(tril, c1) + _dot(tril, c2) + _dot(tril, c3)
        r1, r2, r3 = _split3(bgt_ref[si, :, rows(ci)])
        gcum_r[si, ci] = _dot(r1, triu) + _dot(r2, triu) + _dot(r3, triu)

    q = each(lambda si, ci, hd: qkv_ref[si, rows(ci), _qkv_head(hd, 0)])
    k = each(lambda si, ci, hd: qkv_ref[si, rows(ci), _qkv_head(hd, 1)])
    v = each(lambda si, ci, hd: qkv_ref[si, rows(ci), _qkv_head(hd, 2)])
    beta = each(lambda si, ci, hd: bgc[si, ci][:, hd:hd + 1])
    gcol = each(lambda si, ci, hd: gcum_c[si, ci][:, N_HEADS + hd:N_HEADS + hd + 1])
    grow = each(lambda si, ci, hd: gcum_r[si, ci][N_HEADS + hd:N_HEADS + hd + 1, :])
    glast = each(lambda si, ci, hd: gcum_c[si, ci][L - 1:L, N_HEADS + hd:N_HEADS + hd + 1])

    decay = each(lambda *it: jnp.where(
        causal, jnp.exp(jnp.where(causal, gcol[it] - grow[it], 0.0)), 0.0))
    kk = each(lambda *it: _dot_nt(k[it], k[it]))
    lm = each(lambda *it: jnp.where(strict, beta[it] * kk[it] * decay[it], 0.0))

    x = each(lambda *it: eye - lm[it])
    pw = lm
    n = 2
    while n < L:
        pb = each(lambda *it: pw[it].astype(BF16))
        pw = each(lambda *it: _dot(pb[it], pb[it]))
        x = each(lambda *it: x[it] + _dot(x[it].astype(BF16), pw[it].astype(BF16)))
        n *= 2

    eg = each(lambda *it: jnp.exp(gcol[it]))
    rhs = each(lambda *it: jnp.concatenate(
        [v[it].astype(F32) * beta[it], k[it].astype(F32) * (beta[it] * eg[it])], axis=1).astype(BF16))
    sol = each(lambda *it: _dot(x[it].astype(BF16), rhs[it]))
    attn = each(lambda *it: (_dot_nt(q[it], k[it]) * decay[it]).astype(BF16))
    q_dec = each(lambda *it: (q[it].astype(F32) * eg[it]).astype(BF16))
    k_dec = each(lambda *it: (k[it].astype(F32) * jnp.exp(glast[it] - gcol[it])).astype(BF16))
    last = each(lambda *it: jnp.exp(glast[it]))
    return sol, attn, (lambda it, u: u), q_dec, k_dec, last


def _intra_chunk_pairs(qkv_ref, bg_ref, bgt_ref, *, L, cps, ns):
    P = 2 * L
    rowp = lax.broadcasted_iota(jnp.int32, (L, P), 0)
    lanep = lax.broadcasted_iota(jnp.int32, (L, P), 1)
    left = lanep < L
    colp = jnp.where(left, lanep, lanep - L)
    causal = rowp >= colp
    strict = rowp > colp
    eye = jnp.where(rowp == colp, 1.0, 0.0).astype(F32)
    r2 = lax.broadcasted_iota(jnp.int32, (P, P), 0)
    c2 = lax.broadcasted_iota(jnp.int32, (P, P), 1)
    same = (r2 < L) == (c2 < L)
    tril = jnp.where(jnp.logical_and(same, r2 >= c2), 1.0, 0.0).astype(BF16)
    triu = jnp.where(jnp.logical_and(same, r2 <= c2), 1.0, 0.0).astype(BF16)

    pairs = [(si, m) for si in range(ns) for m in range(cps // 2)]
    items = [(si, m, hd) for (si, m) in pairs for hd in range(N_HEADS)]
    each = lambda f: {it: f(*it) for it in items}
    rows = lambda m: slice(m * P, (m + 1) * P)
    side_by_side = lambda a: jnp.where(left, a[0:L], a[L:P])
    block_diag = lambda a: jnp.concatenate(
        [jnp.where(left, a, 0.0), jnp.where(left, 0.0, a)], axis=0).astype(BF16)

    bgc, gcum_c, gcum_r = {}, {}, {}
    for si, m in pairs:
        bgc[si, m] = bg_ref[si, rows(m), :]
        c1, c2_, c3 = _split3(bgc[si, m])
        gcum_c[si, m] = _dot(tril, c1) + _dot(tril, c2_) + _dot(tril, c3)
        r1, r2_, r3 = _split3(bgt_ref[si, :, rows(m)])
        gcum_r[si, m] = _dot(r1, triu) + _dot(r2_, triu) + _dot(r3, triu)

    q = each(lambda si, m, hd: qkv_ref[si, rows(m), _qkv_head(hd, 0)])
    k = each(lambda si, m, hd: qkv_ref[si, rows(m), _qkv_head(hd, 1)])
    v = each(lambda si, m, hd: qkv_ref[si, rows(m), _qkv_head(hd, 2)])
    beta = each(lambda si, m, hd: bgc[si, m][:, hd:hd + 1])
    gcol = each(lambda si, m, hd: gcum_c[si, m][:, N_HEADS + hd:N_HEADS + hd + 1])
    grow = each(lambda si, m, hd: gcum_r[si, m][N_HEADS + hd:N_HEADS + hd + 1, :])

    decay = each(lambda *it: jnp.where(
        causal, jnp.exp(jnp.where(causal, side_by_side(gcol[it]) - grow[it], 0.0)), 0.0))
    kq = each(lambda *it: _dot_nt(jnp.concatenate([k[it], q[it]], axis=0), k[it]))
    kk = each(lambda *it: side_by_side(kq[it][0:P]))
    lm = each(lambda *it: jnp.where(strict, side_by_side(beta[it]) * kk[it] * decay[it], 0.0))

    x = each(lambda *it: eye - lm[it])
    pw = each(lambda *it: _dot(lm[it].astype(BF16), block_diag(lm[it])))
    n = 2
    while n < L:
        w = each(lambda *it: block_diag(pw[it]))
        if 2 * n < L:
            r = each(lambda *it: _dot(jnp.concatenate([x[it], pw[it]], axis=0).astype(BF16), w[it]))
            x = each(lambda *it: x[it] + r[it][0:L])
            pw = each(lambda *it: r[it][L:P])
        else:
            x = each(lambda *it: x[it] + _dot(x[it].astype(BF16), w[it]))
        n *= 2

    eg = each(lambda *it: jnp.exp(gcol[it]))
    rhs = each(lambda *it: jnp.concatenate(
        [v[it].astype(F32) * beta[it], k[it].astype(F32) * (beta[it] * eg[it])], axis=1).astype(BF16))
    sol2 = each(lambda *it: _dot(block_diag(x[it]), rhs[it]))
    attn2 = each(lambda *it: side_by_side(kq[it][P:2 * P]) * decay[it])
    q_dec2 = each(lambda *it: (q[it].astype(F32) * eg[it]).astype(BF16))

    sol, attn, q_dec, k_dec, last = {}, {}, {}, {}, {}
    for it in items:
        si, m, hd = it
        for half in range(2):
            key = (si, 2 * m + half, hd)
            hr = slice(half * L, (half + 1) * L)
            glast = gcum_c[si, m][(half + 1) * L - 1:(half + 1) * L, N_HEADS + hd:N_HEADS + hd + 1]
            sol[key] = sol2[it][hr]
            attn[key] = jnp.where(left if half == 0 else jnp.logical_not(left), attn2[it], 0.0).astype(BF16)
            q_dec[key] = q_dec2[it][hr]
            k_dec[key] = (k[it][hr].astype(F32) * jnp.exp(glast - gcol[it][hr])).astype(BF16)
            last[key] = jnp.exp(glast)

    def attn_rhs(key, u):
        z = jnp.zeros_like(u)
        return jnp.concatenate([u, z] if key[1] % 2 == 0 else [z, u], axis=0)

    return sol, attn, attn_rhs, q_dec, k_dec, last


def _delta_kernel(qkv_ref, gz_ref, bg_ref, bgt_ref, s0_ref, ghead_ref, og_ref, sout_ref, s_ref,
                  *, L, cps, ns):
    c = pl.program_id(1)
    nc = pl.num_programs(1)

    @pl.when(c == 0)
    def _():
        for si in range(ns):
            s_ref[si] = s0_ref[min(si, s0_ref.shape[0] - 1)]

    ghead = ghead_ref[...]
    rows = lambda ci: slice(ci * L, (ci + 1) * L)
    head = lambda hd: slice(hd * D_HEAD, (hd + 1) * D_HEAD)
    intra = _intra_chunk_pairs if (cps % 2 == 0 and 2 * L == LANES) else _intra_chunks
    sol, attn, attn_rhs, q_dec, k_dec, last = intra(qkv_ref, bg_ref, bgt_ref, L=L, cps=cps, ns=ns)

    state = {(si, hd): s_ref[si, hd] for si in range(ns) for hd in range(N_HEADS)}
    for ci in range(cps):
        cur = [(si, ci, hd) for si in range(ns) for hd in range(N_HEADS)]
        now = lambda f: {it: f(*it) for it in cur}
        sb = now(lambda si, ci, hd: state[si, hd].astype(BF16))
        ks = now(lambda *it: _dot(jnp.concatenate(
            [sol[it][:, D_HEAD:].astype(BF16), q_dec[it]], axis=0), sb[it]))
        ub = now(lambda *it: (sol[it][:, :D_HEAD] - ks[it][0:L]).astype(BF16))
        o = now(lambda *it: ks[it][L:2 * L] + _dot(attn[it], attn_rhs(it, ub[it])))
        for it in cur:
            si, _, hd = it
            state[si, hd] = state[si, hd] * last[it] + _dot_tn(k_dec[it], ub[it])
            on = _rms(o[it], ghead) * gz_ref[si, rows(ci), head(hd)].astype(F32)
            og_ref[si, rows(ci), head(hd)] = on.astype(BF16)
    for (si, hd), s in state.items():
        s_ref[si, hd] = s

    @pl.when(c == nc - 1)
    def _():
        sout_ref[...] = s_ref[...]


def _delta(qkv, gz, bg, bgt, s0, ghead, *, L, cps, ns, gz_col=0):
    B, T, _ = qkv.shape
    tb = L * cps
    bs = s0.shape[0]
    assert bs == B or (bs == 1 and ns == 1)
    s_spec = (pl.BlockSpec((ns, N_HEADS, D_HEAD, D_HEAD), lambda b, c: (b, 0, 0, 0)) if bs == B else
              pl.BlockSpec((1, N_HEADS, D_HEAD, D_HEAD), lambda b, c: (0, 0, 0, 0)))
    tile = lambda n, col=0: pl.BlockSpec((ns, tb, n), lambda b, c: (b, c, col))
    return pl.pallas_call(
        functools.partial(_delta_kernel, L=L, cps=cps, ns=ns),
        grid=(B // ns, T // tb),
        in_specs=[
            tile(D_QKV), tile(D_MODEL, gz_col), tile(LANES),
            pl.BlockSpec((ns, 2 * N_HEADS, tb), lambda b, c: (b, 0, c)),
            s_spec,
            _resident((1, D_HEAD)),
        ],
        out_specs=[
            tile(D_MODEL),
            pl.BlockSpec((ns, N_HEADS, D_HEAD, D_HEAD), lambda b, c: (b, 0, 0, 0)),
        ],
        out_shape=[
            jax.ShapeDtypeStruct((B, T, D_MODEL), BF16),
            jax.ShapeDtypeStruct((B, N_HEADS, D_HEAD, D_HEAD), F32),
        ],
        scratch_shapes=[pltpu.VMEM((ns, N_HEADS, D_HEAD, D_HEAD), F32)],
        compiler_params=pltpu.CompilerParams(
            dimension_semantics=("arbitrary", "arbitrary"), vmem_limit_bytes=VMEM_LIMIT),
        name="delta_rule",
    )(qkv, gz, bg, bgt, s0, ghead)


def _mixer_out_kernel(x_ref, a_ref, og_ref, sgd_ref, wdout_ref, wo_ref, gmlp_ref, wup_ref, wdown_ref,
                      gfin_ref, y_ref):
    dn = _dot(og_ref[...], wdout_ref[...])
    mix = a_ref[...].astype(F32) + sgd_ref[...].astype(F32) * dn
    x2 = x_ref[...] + _dot(mix.astype(BF16), wo_ref[...])
    hn = _rms(x2, gmlp_ref[...]).astype(BF16)
    acc = x2
    for j in range(0, D_FF, D_MODEL):
        hm = jnp.maximum(_dot(hn, wup_ref[:, j:j + D_MODEL]), 0.0)
        acc = acc + _dot((hm * hm).astype(BF16), wdown_ref[j:j + D_MODEL, :])
    y_ref[...] = _rms(acc, gfin_ref[...])


def _mixer_out(x, a, og, sgd, p, *, tm, a_col=0, sgd_col=0):
    n = x.shape[0]
    tile = pl.BlockSpec((tm, D_MODEL), lambda i: (i, 0))
    col_tile = lambda col: pl.BlockSpec((tm, D_MODEL), lambda i: (i, col))
    return pl.pallas_call(
        _mixer_out_kernel,
        grid=(n // tm,),
        in_specs=[tile, col_tile(a_col), tile, col_tile(sgd_col),
                  _resident(p["wdout"].shape), _resident(p["wo"].shape), _resident((1, D_MODEL)),
                  _resident(p["wup"].shape), _resident(p["wdown"].shape), _resident((1, D_MODEL))],
        out_specs=tile,
        out_shape=jax.ShapeDtypeStruct((n, D_MODEL), F32),
        compiler_params=pltpu.CompilerParams(
            dimension_semantics=("arbitrary",), vmem_limit_bytes=VMEM_LIMIT),
        name="mixer_out",
    )(x, a, og, sgd, p["wdout"], p["wo"], p["gmlp"], p["wup"], p["wdown"], p["gfin"])


def _prep_params(g_mix, w_in, w_dw, b_dw, ln_g, ln_b, w_cout, w_short, a_log, dt_bias, g_head,
                 w_dout, w_o, g_mlp, w_up, w_down, g_final):
    w = w_in[0]
    n_main = 2 * D_MODEL + D_QKV + D_MODEL
    ba = w[:, n_main:n_main + 2 * N_HEADS]
    pad16 = lambda v: jnp.pad(v.astype(F32), (N_HEADS, 0))
    rows8 = lambda v: jnp.broadcast_to(v[:, None, :], (v.shape[0], SUBLANES, v.shape[1]))
    return dict(
        wmain=jnp.concatenate([w[:, :n_main], w[:, n_main + 2 * N_HEADS:]], axis=1).astype(BF16),
        wba=jnp.pad(ba, ((0, 0), (0, LANES - 2 * N_HEADS))).astype(BF16),
        wbat=ba.T.astype(BF16),
        gmix=g_mix[0].reshape(1, D_MODEL),
        wdw=w_dw[0], wdw8=rows8(w_dw[0]), bdw=b_dw[0].reshape(1, D_MODEL),
        lng=ln_g[0].reshape(1, D_MODEL), lnb=ln_b[0].reshape(1, D_MODEL),
        wcout=w_cout[0].astype(BF16), wsh=w_short[0], wsh8=rows8(w_short[0]),
        alog_row=jnp.pad(pad16(a_log[0]), (0, LANES - 2 * N_HEADS)).reshape(1, LANES),
        dt_row=jnp.pad(pad16(dt_bias[0]), (0, LANES - 2 * N_HEADS)).reshape(1, LANES),
        alog_col=pad16(a_log[0]).reshape(2 * N_HEADS, 1),
        dt_col=pad16(dt_bias[0]).reshape(2 * N_HEADS, 1),
        ghead=g_head[0].reshape(1, D_HEAD),
        wdout=w_dout[0].astype(BF16), wo=w_o[0].astype(BF16), gmlp=g_mlp[0].reshape(1, D_MODEL),
        wup=w_up[0].astype(BF16), wdown=w_down[0].astype(BF16), gfin=g_final.reshape(1, D_MODEL),
    )


def kernel(x_prompt, x_sample, state_conv, state_qkv_conv, state_delta, meta_tokens, g_mix, w_in, w_dw, b_dw, ln_g, ln_b, w_cout, w_short, a_log, dt_bias, g_head, w_dout, w_o, g_mlp, w_up, w_down, g_final):
    p = _prep_params(g_mix, w_in, w_dw, b_dw, ln_g, ln_b, w_cout, w_short, a_log, dt_bias, g_head,
                     w_dout, w_o, g_mlp, w_up, w_down, g_final)
    bp, seq, _ = x_prompt.shape
    bd, ts, _ = x_sample.shape

    zc = jnp.zeros((1, CONV_HALO, D_MODEL), F32)
    zq = jnp.zeros((1, SHORT_HALO, D_QKV), F32)
    zs = jnp.zeros((1, N_HEADS, D_HEAD, D_HEAD), F32)
    xm = meta_tokens.astype(F32)[None]
    gz_col, sgd_col, a_col = ACT_GZ // D_MODEL, ACT_SGD // D_MODEL, ACT_A // D_MODEL
    act_m, bg_m, bgt_m, cv_m, qk_m = _mixer_in_seq(xm, zc, zq, p, tm=N_META, row_block=N_META)
    _, s_m = _delta(act_m, act_m, bg_m, bgt_m, zs, p["ghead"], L=N_META, cps=1, ns=1, gz_col=gz_col)

    act_p, bg_p, bgt_p, cv_p, qk_p = _mixer_in_seq(
        x_prompt, cv_m, qk_m, p, tm=TM_IN_PROMPT, row_block=CONV_ROW_BLOCK)
    og_p, s_p = _delta(act_p, act_p, bg_p, bgt_p, s_m, p["ghead"], L=DELTA_CHUNK,
                       cps=DELTA_CHUNKS_PER_STEP, ns=1, gz_col=gz_col)
    n_p = bp * seq
    act_flat = act_p.reshape(n_p, ACT_WIDTH)
    y_prompt = _mixer_out(x_prompt.reshape(n_p, D_MODEL), act_flat, og_p.reshape(n_p, D_MODEL), act_flat,
                          p, tm=TM_OUT_PROMPT, a_col=a_col, sgd_col=sgd_col).reshape(bp, seq, D_MODEL)

    tmaj = lambda v: jnp.transpose(v, (1, 0, 2))
    a_s, qkv_s, gz_s, sgd_s, bg_s, glu_s, raw_s = _mixer_in_dec(
        tmaj(x_sample), tmaj(state_conv[0]), tmaj(state_qkv_conv[0]), p, nb=DEC_BATCH_BLOCK)
    padt = lambda v: jnp.pad(tmaj(v), ((0, 0), (0, DEC_PAD - ts), (0, 0)))
    bgt_s = jnp.pad(jnp.transpose(bg_s[:, :, :2 * N_HEADS], (1, 2, 0)), ((0, 0), (0, 0), (0, DEC_PAD - ts)))
    og_s, s_s = _delta(padt(qkv_s), padt(gz_s), padt(bg_s), bgt_s, state_delta[0], p["ghead"],
                       L=DEC_PAD, cps=1, ns=DEC_SEQS_PER_STEP)
    n_s = bd * ts
    y_sample = _mixer_out(x_sample.reshape(n_s, D_MODEL), tmaj(a_s).reshape(n_s, D_MODEL),
                          og_s[:, :ts].reshape(n_s, D_MODEL), tmaj(sgd_s).reshape(n_s, D_MODEL), p,
                          tm=TM_OUT_DECODE).reshape(bd, ts, D_MODEL)

    nbuf, nsh = K_CONV - 1, K_SHORT - 1
    prompt_conv = cv_p[:, CONV_HALO - nbuf:][None]
    prompt_qkv_conv = qk_p[:, SHORT_HALO - nsh:][None]
    sample_conv = jnp.concatenate([state_conv[0][:, ts:], tmaj(glu_s)], axis=1)[None]
    sample_qkv_conv = tmaj(raw_s[ts - nsh:])[None]
    return (y_prompt, y_sample, prompt_conv, prompt_qkv_conv, s_p[None],
            sample_conv, sample_qkv_conv, s_s[None])
```
